```python
import math
import jax
import jax.numpy as jnp
from jax import lax
import numpy as np

D_MODEL = 2048
BATCH = 2
SEQ = 4096
DEPTH = 4

CTX_LEN = 256
GRID_W = 64
N_MOD = 6
EPS = 1e-6
N_BRANCH = 3
BRANCH_WIDTH = D_MODEL // 2
RG_WIDTH = BRANCH_WIDTH
RG_BLOCKS = 8
RG_BLOCK_DIM = RG_WIDTH // RG_BLOCKS
RG_CONV_W = 4
RG_CONV_LEFT = 2
RG_C = 8.0
DA_HEADS = 8
DA_HEAD_DIM = BRANCH_WIDTH // (2 * DA_HEADS)
DA_WIDTH = DA_HEADS * 2 * DA_HEAD_DIM
ROPE_AXIS_DIM = DA_HEAD_DIM // 2
ROPE_BASE = 10000.0
Q_BLOCK = 128
HG_HEADS = 8
HG_DK = BRANCH_WIDTH // HG_HEADS
HG_DV = BRANCH_WIDTH // HG_HEADS
HG_WIDTH = HG_HEADS * HG_DV
HG_CHUNK = 64
LOGF_FLOOR = 1e-20
PEER_HEADS = 8
PEER_NKEYS = 128
PEER_N = PEER_NKEYS * PEER_NKEYS
PEER_DQ = 256
PEER_TOPK = 16
PEER_TOK_BLOCK = 128

IN_NAMES = ('rg_x', 'rg_gate', 'da_q', 'da_k', 'da_v', 'hg_q', 'hg_f_fwd', 'hg_f_bwd', 'hg_i', 'hg_g', 'gates')
IN_WIDTHS = (RG_WIDTH, RG_WIDTH, 2 * DA_HEADS * DA_HEAD_DIM, 2 * DA_HEADS * DA_HEAD_DIM, DA_WIDTH,
             HG_HEADS * HG_DK, HG_HEADS * HG_DK, HG_HEADS * HG_DK, HG_WIDTH, HG_WIDTH, N_BRANCH * D_MODEL)
IN_WIDTH = sum(IN_WIDTHS)
IN_OFFSETS = tuple(int(v) for v in np.cumsum(IN_WIDTHS)[:-1])

kernel_name = 'hybrid_diffusion_rglru_diffattn_hgrn2_peer'


def rms_norm(x, g):
    xf = x.astype(jnp.float32)
    y = xf * lax.rsqrt(jnp.mean(xf * xf, axis=-1, keepdims=True) + EPS)
    return (y * g).astype(x.dtype)


def modulate(h, shift, scale):
    return h * (1 + scale) + shift


def axial_rope_tables(rows):
    r = jnp.repeat(jnp.arange(rows, dtype=jnp.float32), GRID_W)
    col = jnp.tile(jnp.arange(GRID_W, dtype=jnp.float32), rows)
    inv = ROPE_BASE ** (-jnp.arange(0, ROPE_AXIS_DIM, 2, dtype=jnp.float32) / ROPE_AXIS_DIM)
    ang = jnp.concatenate([r[:, None] * inv, col[:, None] * inv], axis=-1)
    return jnp.cos(ang), jnp.sin(ang)


def apply_rope(x, cos, sin):
    half = x.shape[-1] // 2
    x1, x2 = x[..., :half], x[..., half:]
    cos = cos.astype(x.dtype)
    sin = sin.astype(x.dtype)
    return jnp.concatenate([x1 * cos - x2 * sin, x1 * sin + x2 * cos], axis=-1)


def short_conv(x, w, b):
    s = x.shape[1]
    xp = jnp.pad(x, ((0, 0), (RG_CONV_LEFT, RG_CONV_W - 1 - RG_CONV_LEFT), (0, 0)))
    y = b
    for k in range(RG_CONV_W):
        y = y + w[k] * xp[:, k:k + s]
    return y


def blockdiag(x, w, b):
    xb = x.reshape(x.shape[0], x.shape[1], RG_BLOCKS, RG_BLOCK_DIM)
    return jnp.einsum('bsnd,nde->bsne', xb, w).reshape(x.shape) + b


def rglru_coeffs(u, wa, ba, wx, bx, lam):
    u = u.astype(jnp.float32)
    r = jax.nn.sigmoid(blockdiag(u, wa, ba))
    i = jax.nn.sigmoid(blockdiag(u, wx, bx))
    log_a = -RG_C * r * jax.nn.softplus(-lam)
    a = jnp.exp(log_a)
    mult = jnp.sqrt(-jnp.expm1(2.0 * log_a))
    return a, mult * (i * u)


def linear_scan(a, b, reverse):
    def comb(left, right):
        a1, b1 = left
        a2, b2 = right
        return a1 * a2, a2 * b1 + b2
    return lax.associative_scan(comb, (a, b), reverse=reverse, axis=1)[1]


def rglru_branch(pc, pl, conv_w, conv_b, wa, ba, wx, bx, lam, with_ctx_out):
    uc = short_conv(pc['rg_x'], conv_w, conv_b)
    ul = short_conv(pl['rg_x'], conv_w, conv_b)
    hs_c, hs_l = [], []
    for d, rev in enumerate((False, True)):
        a_c, b_c = rglru_coeffs(uc, wa[d], ba[d], wx[d], bx[d], lam[d])
        h_c = linear_scan(a_c, b_c, rev)
        h0 = h_c[:, 0] if rev else h_c[:, -1]
        a_l, b_l = rglru_coeffs(ul, wa[d], ba[d], wx[d], bx[d], lam[d])
        first = -1 if rev else 0
        b_l = b_l.at[:, first].add(a_l[:, first] * h0)
        hs_c.append(h_c)
        hs_l.append(linear_scan(a_l, b_l, rev))
    y_l = (hs_l[0] + hs_l[1]).astype(ul.dtype) * jax.nn.gelu(pl['rg_gate'])
    y_c = (hs_c[0] + hs_c[1]).astype(uc.dtype) * jax.nn.gelu(pc['rg_gate']) if with_ctx_out else None
    return y_c, y_l


def da_heads_qk(t):
    b, s, _ = t.shape
    return t.reshape(b, s, DA_HEADS, 2, DA_HEAD_DIM).transpose(0, 2, 3, 1, 4)


def da_heads_v(t):
    b, s, _ = t.shape
    return t.reshape(b, s, DA_HEADS, 2 * DA_HEAD_DIM).transpose(0, 2, 1, 3)


def diff_softmax_attend(q, k, v, lam):
    s = jnp.einsum('bhjqd,bhjkd->bhjqk', q, k).astype(jnp.float32)
    p = jax.nn.softmax(s, axis=-1)
    w = p[:, :, 0] - lam * p[:, :, 1]
    return jnp.einsum('bhqk,bhkv->bhqv', w.astype(v.dtype), v)


def diff_attn_branch(pc, pl, lq, lk, subln_g, lam_init, cos, sin, with_ctx_out):
    scale = DA_HEAD_DIM ** -0.5
    lam = jnp.exp(jnp.sum(lq[0] * lk[0])) - jnp.exp(jnp.sum(lq[1] * lk[1])) + lam_init
    qc = da_heads_qk(pc['da_q']) * scale
    kc = da_heads_qk(pc['da_k'])
    vc = da_heads_v(pc['da_v'])
    ql = apply_rope(da_heads_qk(pl['da_q']), cos, sin) * scale
    kl = apply_rope(da_heads_qk(pl['da_k']), cos, sin)
    vl = da_heads_v(pl['da_v'])
    keys = jnp.concatenate([kc, kl], axis=3)
    vals = jnp.concatenate([vc, vl], axis=2)
    b, h, _, s, d = ql.shape
    nb = s // Q_BLOCK
    q_blocks = ql.reshape(b, h, 2, nb, Q_BLOCK, d).transpose(3, 0, 1, 2, 4, 5)
    o_l = lax.map(lambda qb: diff_softmax_attend(qb, keys, vals, lam), q_blocks)
    o_l = o_l.transpose(1, 2, 0, 3, 4).reshape(b, h, s, 2 * d)

    def finish(o):
        o = rms_norm(o, subln_g) * (1.0 - lam_init)
        return o.transpose(0, 2, 1, 3).reshape(o.shape[0], o.shape[2], DA_WIDTH)

    y_l = finish(o_l)
    y_c = finish(diff_softmax_attend(qc, kc, vc, lam)) if with_ctx_out else None
    return y_c, y_l


def hg_heads(t):
    b, s, w = t.shape
    return t.reshape(b, s, HG_HEADS, w // HG_HEADS).transpose(0, 2, 1, 3)


def hgrn_chunk_scan(q, logf, k, v, s0):
    b, h, t, _ = q.shape
    dv = v.shape[-1]
    n = t // HG_CHUNK

    def chunks(a):
        return a.reshape(b, h, n, HG_CHUNK, a.shape[-1]).transpose(2, 0, 1, 3, 4)

    tri = jnp.tril(jnp.ones((HG_CHUNK, HG_CHUNK), dtype=bool))[:, :, None]

    def step(s, inp):
        qc, gc, kc, vc = inp
        cum = jnp.cumsum(gc, axis=2)
        rel = cum[:, :, :, None, :] - cum[:, :, None, :, :]
        decay = jnp.exp(jnp.where(tri, rel, -jnp.inf))
        attn = jnp.einsum('bhtk,bhsk,bhtsk->bhts', qc, kc, decay)
        o = jnp.einsum('bhts,bhsv->bhtv', attn, vc) + jnp.einsum('bhtk,bhkv->bhtv', qc * jnp.exp(cum), s)
        c_end = cum[:, :, -1:, :]
        s = jnp.exp(c_end[:, :, 0, :, None]) * s + jnp.einsum('bhsk,bhsv->bhkv', kc * jnp.exp(c_end - cum), vc)
        return s, o

    s_end, o = lax.scan(step, s0, (chunks(q), chunks(logf), chunks(k), chunks(v)))
    return o.transpose(1, 2, 0, 3, 4).reshape(b, h, t, dv), s_end


def hgrn_branch(pc, pl, lb, onorm_g, with_ctx_out):
    bsz = pl['hg_q'].shape[0]
    outs_c, outs_l = [], []
    for d, (fname, rev) in enumerate((('hg_f_fwd', False), ('hg_f_bwd', True))):
        def prep(p):
            z = p[fname].astype(jnp.float32)
            f = lb[d] + (1.0 - lb[d]) * jax.nn.sigmoid(z)
            k = (1.0 - lb[d]) * jax.nn.sigmoid(-z)
            logf = jnp.log(jnp.maximum(f, LOGF_FLOOR))
            ts = [hg_heads(a.astype(jnp.float32)) for a in (p['hg_q'], logf, k, p['hg_i'])]
            return [jnp.flip(a, axis=2) for a in ts] if rev else ts
        s0 = jnp.zeros((bsz, HG_HEADS, HG_DK, HG_DV), jnp.float32)
        o_c, s_c = hgrn_chunk_scan(*prep(pc), s0)
        o_l, _ = hgrn_chunk_scan(*prep(pl), s_c)
        if rev:
            o_c = jnp.flip(o_c, axis=2)
            o_l = jnp.flip(o_l, axis=2)
        outs_c.append(o_c)
        outs_l.append(o_l)

    def finish(o, g):
        o = rms_norm(o, onorm_g)
        return o.transpose(0, 2, 1, 3).reshape(g.shape).astype(g.dtype) * jax.nn.silu(g)

    y_l = finish(outs_l[0] + outs_l[1], pl['hg_g'])
    y_c = finish(outs_c[0] + outs_c[1], pc['hg_g']) if with_ctx_out else None
    return y_c, y_l


def merge_branches(ys, gate_pre, w_branch, b_gate, w_out):
    gp = gate_pre.reshape(gate_pre.shape[0], gate_pre.shape[1], N_BRANCH, D_MODEL) + b_gate
    g = jax.nn.sigmoid(gp.astype(jnp.float32)).astype(gate_pre.dtype)
    merged = g[:, :, 0] * (ys[0] @ w_branch[0])
    for kb in range(1, N_BRANCH):
        merged = merged + g[:, :, kb] * (ys[kb] @ w_branch[kb])
    return merged @ w_out


def token_mixer(hc, hl, w_in, rg_conv_w, rg_conv_b, rg_wa, rg_ba, rg_wx, rg_bx, rg_lambda,
                da_lq, da_lk, da_subln_g, lam_init, hg_lb, hg_onorm_g, w_branch, b_gate, w_out,
                cos, sin, with_ctx_out):
    pc = dict(zip(IN_NAMES, jnp.split(hc @ w_in, IN_OFFSETS, axis=-1)))
    pl = dict(zip(IN_NAMES, jnp.split(hl @ w_in, IN_OFFSETS, axis=-1)))
    rg_c, rg_l = rglru_branch(pc, pl, rg_conv_w, rg_conv_b, rg_wa, rg_ba, rg_wx, rg_bx, rg_lambda, with_ctx_out)
    da_c, da_l = diff_attn_branch(pc, pl, da_lq, da_lk, da_subln_g, lam_init, cos, sin, with_ctx_out)
    hg_c, hg_l = hgrn_branch(pc, pl, hg_lb, hg_onorm_g, with_ctx_out)
    y_l = merge_branches((rg_l, da_l, hg_l), pl['gates'], w_branch, b_gate, w_out)
    y_c = merge_branches((rg_c, da_c, hg_c), pc['gates'], w_branch, b_gate, w_out) if with_ctx_out else None
    return y_c, y_l


def peer_ffn(h, wq, subkeys, u, v):
    t, d = h.shape
    q = (h @ wq).reshape(t, PEER_HEADS, 2, PEER_DQ // 2)
    s = jnp.einsum('thjd,jnd->thjn', q, subkeys).astype(jnp.float32)
    s1, i1 = lax.top_k(s[:, :, 0], PEER_TOPK)
    s2, i2 = lax.top_k(s[:, :, 1], PEER_TOPK)
    cand_s = (s1[..., :, None] + s2[..., None, :]).reshape(t, PEER_HEADS, PEER_TOPK * PEER_TOPK)
    cand_i = (i1[..., :, None] * PEER_NKEYS + i2[..., None, :]).reshape(t, PEER_HEADS, PEER_TOPK * PEER_TOPK)
    top_s, pos = lax.top_k(cand_s, PEER_TOPK)
    idx = jnp.take_along_axis(cand_i, pos, axis=-1)
    w = jax.nn.softmax(top_s, axis=-1).astype(h.dtype)
    nb = t // PEER_TOK_BLOCK

    def block(args):
        hb, ib, wb = args
        act = jax.nn.gelu(jnp.einsum('td,thkd->thk', hb, u[ib]))
        return jnp.einsum('thk,thkd->td', wb * act, v[ib])

    y = lax.map(block, (h.reshape(nb, PEER_TOK_BLOCK, d),
                        idx.reshape(nb, PEER_TOK_BLOCK, PEER_HEADS, PEER_TOPK),
                        w.reshape(nb, PEER_TOK_BLOCK, PEER_HEADS, PEER_TOPK)))
    return y.reshape(t, d)


def setup_inputs(seed: int = 0) -> dict:
    key = jax.random.key(seed)
    ks = jax.random.split(key, 32)
    f32 = jnp.float32
    D = D_MODEL

    def nrm(k, shape, std):
        return jax.random.normal(k, shape, f32) * std

    p_lo, p_hi = 0.9 ** (1.0 / RG_C), 0.999 ** (1.0 / RG_C)
    p = jax.random.uniform(ks[15], (DEPTH, 2, RG_WIDTH), f32, p_lo, p_hi)
    rg_lambda = jnp.log(p) - jnp.log1p(-p)
    return {
        'x': nrm(ks[0], (BATCH, SEQ, D), 1.0),
        'c': nrm(ks[1], (BATCH, D), 1.0),
        'ctx': nrm(ks[2], (BATCH, CTX_LEN, D), 1.0),
        'c_ctx': nrm(ks[3], (D,), 1.0),
        'w_ada': nrm(ks[4], (DEPTH, D, N_MOD * D), 0.5 * D ** -0.5),
        'b_ada': nrm(ks[5], (DEPTH, N_MOD * D), 0.02),
        'norm_mix_g': 1.0 + nrm(ks[6], (DEPTH, D), 0.02),
        'norm_ffn_g': 1.0 + nrm(ks[7], (DEPTH, D), 0.02),
        'w_in': nrm(ks[8], (DEPTH, D, IN_WIDTH), D ** -0.5),
        'rg_conv_w': nrm(ks[9], (DEPTH, RG_CONV_W, RG_WIDTH), RG_CONV_W ** -0.5),
        'rg_conv_b': nrm(ks[10], (DEPTH, RG_WIDTH), 0.02),
        'rg_wa': nrm(ks[11], (DEPTH, 2, RG_BLOCKS, RG_BLOCK_DIM, RG_BLOCK_DIM), RG_BLOCK_DIM ** -0.5),
        'rg_ba': nrm(ks[12], (DEPTH, 2, RG_WIDTH), 0.02),
        'rg_wx': nrm(ks[13], (DEPTH, 2, RG_BLOCKS, RG_BLOCK_DIM, RG_BLOCK_DIM), RG_BLOCK_DIM ** -0.5),
        'rg_bx': nrm(ks[14], (DEPTH, 2, RG_WIDTH), 0.02),
        'rg_lambda': rg_lambda,
        'da_lq': nrm(ks[16], (DEPTH, 2, DA_HEAD_DIM), 0.1),
        'da_lk': nrm(ks[17], (DEPTH, 2, DA_HEAD_DIM), 0.1),
        'da_subln_g': 1.0 + nrm(ks[18], (DEPTH, 2 * DA_HEAD_DIM), 0.02),
        'hg_lb': nrm(ks[19], (2, DEPTH, HG_HEADS * HG_DK), 0.5),
        'hg_onorm_g': 1.0 + nrm(ks[20], (DEPTH, HG_DV), 0.02),
        'w_branch': nrm(ks[21], (DEPTH, N_BRANCH, BRANCH_WIDTH, D), BRANCH_WIDTH ** -0.5),
        'b_gate': nrm(ks[22], (DEPTH, N_BRANCH, D), 0.02),
        'w_out': nrm(ks[23], (DEPTH, D, D), D ** -0.5),
        'peer_wq': nrm(ks[24], (DEPTH, D, PEER_HEADS * PEER_DQ), D ** -0.5),
        'peer_subkeys': nrm(ks[25], (DEPTH, 2, PEER_NKEYS, PEER_DQ // 2), (PEER_DQ // 2) ** -0.5),
        'peer_u': nrm(ks[26], (DEPTH, PEER_N, D), D ** -0.5),
        'peer_v': nrm(ks[27], (DEPTH, PEER_N, D), 0.5 * PEER_HEADS ** -0.5),
        'final_norm_g': 1.0 + nrm(ks[28], (D,), 0.02),
    }


def reference(x, c, ctx, c_ctx, w_ada, b_ada, norm_mix_g, norm_ffn_g, w_in, rg_conv_w, rg_conv_b,
              rg_wa, rg_ba, rg_wx, rg_bx, rg_lambda, da_lq, da_lk, da_subln_g, hg_lb, hg_onorm_g,
              w_branch, b_gate, w_out, peer_wq, peer_subkeys, peer_u, peer_v, final_norm_g):
    bsz, s, d = x.shape
    n_ctx = ctx.shape[1]
    rows = s // GRID_W
    cos, sin = axial_rope_tables(rows)
    lb = jnp.cumsum(jax.nn.softmax(hg_lb.astype(jnp.float32), axis=1), axis=1)
    lb = lb - lb[:, :1]
    sc_l = jax.nn.silu(c)
    sc_c = jax.nn.silu(c_ctx)
    xl, xc = x, ctx
    for l in range(DEPTH):
        last = l == DEPTH - 1
        mod_l = [m[:, None, :] for m in jnp.split(sc_l @ w_ada[l] + b_ada[l], N_MOD, axis=-1)]
        mod_c = jnp.split(sc_c @ w_ada[l] + b_ada[l], N_MOD, axis=-1)
        hl = modulate(rms_norm(xl, norm_mix_g[l]), mod_l[0], mod_l[1])
        hc = modulate(rms_norm(xc, norm_mix_g[l]), mod_c[0], mod_c[1])
        yc, yl = token_mixer(hc, hl, w_in[l], rg_conv_w[l], rg_conv_b[l], rg_wa[l], rg_ba[l], rg_wx[l],
                             rg_bx[l], rg_lambda[l], da_lq[l], da_lk[l], da_subln_g[l],
                             0.8 - 0.6 * math.exp(-0.3 * l), lb[:, l], hg_onorm_g[l], w_branch[l],
                             b_gate[l], w_out[l], cos, sin, not last)
        xl = xl + mod_l[2] * yl
        if not last:
            xc = xc + mod_c[2] * yc
        hl = modulate(rms_norm(xl, norm_ffn_g[l]), mod_l[3], mod_l[4])
        if last:
            yl = peer_ffn(hl.reshape(bsz * s, d), peer_wq[l], peer_subkeys[l], peer_u[l], peer_v[l]).reshape(bsz, s, d)
        else:
            hc = modulate(rms_norm(xc, norm_ffn_g[l]), mod_c[3], mod_c[4])
            h_all = jnp.concatenate([hc, hl], axis=1)
            y_all = peer_ffn(h_all.reshape(-1, d), peer_wq[l], peer_subkeys[l], peer_u[l],
                             peer_v[l]).reshape(bsz, n_ctx + s, d)
            xc = xc + mod_c[5] * y_all[:, :n_ctx]
            yl = y_all[:, n_ctx:]
        xl = xl + mod_l[5] * yl
    return rms_norm(xl, final_norm_g)
```

```python
import functools
import math

import jax
import jax.numpy as jnp
from jax import lax
from jax.experimental import pallas as pl
from jax.experimental.pallas import tpu as pltpu

F32 = jnp.float32
BF16 = jnp.bfloat16

N_MOD = 6
EPS = 1e-6
GRID_W = 64
ROPE_BASE = 10000.0
RG_C = 8.0
LOGF_FLOOR = 1e-20
HG_CHUNK = 64
PEER_TOPK = 16
BRANCH_COLS = 1024
LANES = 128
SUBLANES = 8
VMEM_LIMIT_BYTES = 56 * 1024 * 1024

COL_RG_X, COL_RG_GATE, COL_DA_Q, COL_DA_K, COL_DA_V, COL_HG_Q, COL_HG_FF, COL_HG_FB, COL_HG_I, COL_HG_G, COL_GATES = range(11)


def _cparams(*sem):
    return pltpu.CompilerParams(dimension_semantics=sem, vmem_limit_bytes=VMEM_LIMIT_BYTES)


def _tile(n, pref):
    t = min(n, pref)
    assert n % t == 0, (n, pref)
    return t


def _gelu(x):
    return 0.5 * x * (1.0 + jnp.tanh(math.sqrt(2.0 / math.pi) * (x + 0.044715 * (x * x * x))))


def _sigmoid(x):
    return 1.0 / (1.0 + jnp.exp(-x))


def _dot(a, b):
    return jnp.dot(a, b, preferred_element_type=F32)


def _dot_nt(a, b):
    return lax.dot_general(a, b, (((1,), (1,)), ((), ())), preferred_element_type=F32)


def _dot_tn(a, b):
    return lax.dot_general(a, b, (((0,), (0,)), ((), ())), preferred_element_type=F32)


def _ada_kernel(c_ref, w_ref, b_ref, o_ref):
    c = c_ref[...]
    sc = c * _sigmoid(c)
    o_ref[...] = _dot(sc.astype(BF16), w_ref[...].astype(BF16)) + b_ref[...]


def ada_mods(cvec, w_ada, b_ada):
    depth, d, n = w_ada.shape
    tn = _tile(n, 1024)
    return pl.pallas_call(
        _ada_kernel,
        grid=(depth, n // tn),
        in_specs=[pl.BlockSpec((SUBLANES, d), lambda l, j: (0, 0)),
                  pl.BlockSpec((None, d, tn), lambda l, j: (l, 0, j)),
                  pl.BlockSpec((None, 1, tn), lambda l, j: (l, 0, j))],
        out_specs=pl.BlockSpec((None, SUBLANES, tn), lambda l, j: (l, 0, j)),
        out_shape=jax.ShapeDtypeStruct((depth, SUBLANES, n), F32),
        compiler_params=_cparams("arbitrary", "arbitrary"),
        name="ada_mods",
    )(cvec, w_ada, b_ada.reshape(depth, 1, n))


def _norm_kernel(x_ref, g_ref, sh_ref, sc_ref, o_ref):
    x = x_ref[...]
    y = x * lax.rsqrt(jnp.mean(x * x, axis=-1, keepdims=True) + EPS) * g_ref[...]
    o_ref[...] = (y * (1.0 + sc_ref[...]) + sh_ref[...]).astype(o_ref.dtype)


def _plain_norm_kernel(x_ref, g_ref, o_ref):
    x = x_ref[...]
    o_ref[...] = (x * lax.rsqrt(jnp.mean(x * x, axis=-1, keepdims=True) + EPS) * g_ref[...]).astype(o_ref.dtype)


def norm_mod(x, g, shift, scale, rows_per_src):
    t, d = x.shape
    tr = _tile(rows_per_src, 512)
    src = lambda i: (i * tr // rows_per_src, 0, 0)
    return pl.pallas_call(
        _norm_kernel,
        grid=(t // tr,),
        in_specs=[pl.BlockSpec((tr, d), lambda i: (i, 0)),
                  pl.BlockSpec((1, d), lambda i: (0, 0)),
                  pl.BlockSpec((None, 1, d), src),
                  pl.BlockSpec((None, 1, d), src)],
        out_specs=pl.BlockSpec((tr, d), lambda i: (i, 0)),
        out_shape=jax.ShapeDtypeStruct((t, d), BF16),
        compiler_params=_cparams("arbitrary"),
        name="norm_mod",
    )(x, g.reshape(1, d), shift, scale)


def plain_norm(x, g):
    t, d = x.shape
    tr = _tile(t, 512)
    return pl.pallas_call(
        _plain_norm_kernel,
        grid=(t // tr,),
        in_specs=[pl.BlockSpec((tr, d), lambda i: (i, 0)),
                  pl.BlockSpec((1, d), lambda i: (0, 0))],
        out_specs=pl.BlockSpec((tr, d), lambda i: (i, 0)),
        out_shape=jax.ShapeDtypeStruct((t, d), x.dtype),
        compiler_params=_cparams("arbitrary"),
        name="final_norm",
    )(x, g.reshape(1, d))


def _mm_kernel(a_ref, w_ref, o_ref):
    o_ref[...] = _dot(a_ref[...], w_ref[...].astype(BF16)).astype(o_ref.dtype)


def matmul(a, w, out_dtype, tm_pref=1024, tn_pref=512):
    m, k = a.shape
    n = w.shape[1]
    tm, tn = _tile(m, tm_pref), _tile(n, tn_pref)
    return pl.pallas_call(
        _mm_kernel,
        grid=(m // tm, n // tn),
        in_specs=[pl.BlockSpec((tm, k), lambda i, j: (i, 0)),
                  pl.BlockSpec((k, tn), lambda i, j: (0, j))],
        out_specs=pl.BlockSpec((tm, tn), lambda i, j: (i, j)),
        out_shape=jax.ShapeDtypeStruct((m, n), out_dtype),
        compiler_params=_cparams("arbitrary", "arbitrary"),
        name="matmul",
    )(a, w)


def _mm_res_kernel(a_ref, w_ref, res_ref, gate_ref, o_ref):
    o_ref[...] = res_ref[...] + gate_ref[...] * _dot(a_ref[...], w_ref[...].astype(BF16))


def matmul_residual(a, w, res, gate, rows_per_src):
    m, k = a.shape
    n = w.shape[1]
    tm, tn = _tile(rows_per_src, 1024), _tile(n, 512)
    return pl.pallas_call(
        _mm_res_kernel,
        grid=(m // tm, n // tn),
        in_specs=[pl.BlockSpec((tm, k), lambda i, j: (i, 0)),
                  pl.BlockSpec((k, tn), lambda i, j: (0, j)),
                  pl.BlockSpec((tm, tn), lambda i, j: (i, j)),
                  pl.BlockSpec((None, 1, tn), lambda i, j: (i * tm // rows_per_src, 0, j))],
        out_specs=pl.BlockSpec((tm, tn), lambda i, j: (i, j)),
        out_shape=jax.ShapeDtypeStruct((m, n), F32),
        compiler_params=_cparams("arbitrary", "arbitrary"),
        name="matmul_residual",
    )(a, w, res, gate)


def _rglru_kernel(xl_ref, gl_ref, xc_ref, gc_ref, cw_ref, cb_ref, wa_ref, ba_ref, wx_ref, bx_ref, lam_ref,
                  yl_ref, yc_ref, al_ref, bl_ref, ac_ref, bc_ref):
    cw = cw_ref[...]
    cb = cb_ref[...]

    def coeffs(x_ref, a_ref, b_ref):
        n = x_ref.shape[0]
        x = x_ref[...]
        row = lax.broadcasted_iota(jnp.int32, x.shape, 0)
        xm2 = jnp.where(row >= 2, pltpu.roll(x, 2, 0), 0.0)
        xm1 = jnp.where(row >= 1, pltpu.roll(x, 1, 0), 0.0)
        xp1 = jnp.where(row < n - 1, pltpu.roll(x, n - 1, 0), 0.0)
        u = cb + cw[0:1] * xm2
        u = u + cw[1:2] * xm1
        u = u + cw[2:3] * x
        u = u + cw[3:4] * xp1
        ub = u.astype(BF16)
        for d in range(2):
            r = _sigmoid(_dot(ub, wa_ref[d].astype(BF16)) + ba_ref[d:d + 1, :])
            i = _sigmoid(_dot(ub, wx_ref[d].astype(BF16)) + bx_ref[d:d + 1, :])
            lam = lam_ref[d:d + 1, :]
            softplus_neg_lam = jnp.maximum(-lam, 0.0) + jnp.log(1.0 + jnp.exp(-jnp.abs(lam)))
            log_a = -RG_C * r * softplus_neg_lam
            a_ref[d] = jnp.exp(log_a)
            th = jnp.tanh(log_a)
            b_ref[d] = jnp.sqrt(-2.0 * th / (1.0 - th)) * (i * u)

    row8 = lax.broadcasted_iota(jnp.int32, (SUBLANES, LANES), 0)

    def scan(a_ref, b_ref, carry):
        nblk = a_ref.shape[1] // SUBLANES

        def body(i, carry):
            cf, cbk = carry
            r0 = pl.multiple_of(i * SUBLANES, SUBLANES)
            a = a_ref[0, pl.ds(r0, SUBLANES), :]
            b = b_ref[0, pl.ds(r0, SUBLANES), :]
            for s in (1, 2, 4):
                m = row8 >= s
                b = jnp.where(m, a * pltpu.roll(b, s, 0) + b, b)
                a = jnp.where(m, a * pltpu.roll(a, s, 0), a)
            h = b + a * cf
            b_ref[0, pl.ds(r0, SUBLANES), :] = h
            cf = h[SUBLANES - 1:SUBLANES, :]
            r1 = pl.multiple_of((nblk - 1 - i) * SUBLANES, SUBLANES)
            a = a_ref[1, pl.ds(r1, SUBLANES), :]
            b = b_ref[1, pl.ds(r1, SUBLANES), :]
            for s in (1, 2, 4):
                m = row8 < SUBLANES - s
                b = jnp.where(m, a * pltpu.roll(b, SUBLANES - s, 0) + b, b)
                a = jnp.where(m, a * pltpu.roll(a, SUBLANES - s, 0), a)
            h = b + a * cbk
            b_ref[1, pl.ds(r1, SUBLANES), :] = h
            cbk = h[0:1, :]
            return cf, cbk

        return lax.fori_loop(0, nblk, body, carry)

    coeffs(xc_ref, ac_ref, bc_ref)
    coeffs(xl_ref, al_ref, bl_ref)
    zero = jnp.zeros((1, LANES), F32)
    carry = scan(ac_ref, bc_ref, (zero, zero))
    scan(al_ref, bl_ref, carry)
    yc_ref[...] = ((bc_ref[0] + bc_ref[1]) * _gelu(gc_ref[...])).astype(yc_ref.dtype)
    yl_ref[...] = ((bl_ref[0] + bl_ref[1]) * _gelu(gl_ref[...])).astype(yl_ref.dtype)


def rglru(proj_l, proj_c, bsz, conv_w, conv_b, wa, ba, wx, bx, lam):
    s = proj_l.shape[0] // bsz
    n_ctx = proj_c.shape[0] // bsz
    nblk, bd = wa.shape[1], wa.shape[2]
    assert bd == LANES
    width = nblk * bd
    cpb = BRANCH_COLS // bd
    row_l = lambda col: pl.BlockSpec((s, bd), lambda b, j: (b, col * cpb + j))
    row_c = lambda col: pl.BlockSpec((n_ctx, bd), lambda b, j: (b, col * cpb + j))
    vec = lambda r: pl.BlockSpec((r, bd), lambda b, j: (0, j))
    wspec = pl.BlockSpec((2, None, bd, bd), lambda b, j: (0, j, 0, 0))
    return pl.pallas_call(
        _rglru_kernel,
        grid=(bsz, nblk),
        in_specs=[row_l(COL_RG_X), row_l(COL_RG_GATE), row_c(COL_RG_X), row_c(COL_RG_GATE),
                  vec(conv_w.shape[0]), vec(1), wspec, vec(2), wspec, vec(2), vec(2)],
        out_specs=[pl.BlockSpec((s, bd), lambda b, j: (b, j)),
                   pl.BlockSpec((n_ctx, bd), lambda b, j: (b, j))],
        out_shape=[jax.ShapeDtypeStruct((bsz * s, width), BF16),
                   jax.ShapeDtypeStruct((bsz * n_ctx, width), BF16)],
        scratch_shapes=[pltpu.VMEM((2, s, bd), F32), pltpu.VMEM((2, s, bd), F32),
                        pltpu.VMEM((2, n_ctx, bd), F32), pltpu.VMEM((2, n_ctx, bd), F32)],
        compiler_params=_cparams("arbitrary", "arbitrary"),
        name="rglru",
    )(proj_l, proj_l, proj_c, proj_c, conv_w, conv_b.reshape(1, width), wa, ba, wx, bx, lam)


def _qkv_prep_kernel(q_ref, k_ref, v_ref, cos_ref, sin_ref, qo_ref, ko_ref, vo_ref, *, scale, rope):
    tr = q_ref.shape[0]
    lane = lax.broadcasted_iota(jnp.int32, (tr, LANES), 1)
    first_half = (lane % 64) < 32
    cos = cos_ref[...]
    sin = sin_ref[...]
    for g in range(q_ref.shape[1] // LANES):
        sl = slice(g * LANES, (g + 1) * LANES)
        for src, dst, mult in ((q_ref, qo_ref, scale), (k_ref, ko_ref, None)):
            x = src[:, sl]
            if rope:
                partner = jnp.where(first_half, pltpu.roll(x, LANES - 32, 1), pltpu.roll(x, 32, 1))
                x = x * cos + partner * sin
            if mult is not None:
                x = x * mult
            dst[:, sl] = x.astype(dst.dtype)
    vo_ref[...] = v_ref[...].astype(vo_ref.dtype)


def qkv_prep(proj, cos, sin, seq, scale, rope):
    t = proj.shape[0]
    tr = _tile(seq, 512)
    col = lambda c: pl.BlockSpec((tr, BRANCH_COLS), lambda i: (i, c))
    tab = pl.BlockSpec((tr, LANES), lambda i: (i % (seq // tr), 0))
    out = jax.ShapeDtypeStruct((t, BRANCH_COLS), BF16)
    return pl.pallas_call(
        functools.partial(_qkv_prep_kernel, scale=scale, rope=rope),
        grid=(t // tr,),
        in_specs=[col(COL_DA_Q), col(COL_DA_K), col(COL_DA_V), tab, tab],
        out_specs=[pl.BlockSpec((tr, BRANCH_COLS), lambda i: (i, 0))] * 3,
        out_shape=[out, out, out],
        compiler_params=_cparams("arbitrary"),
        name="qkv_prep",
    )(proj, proj, proj, cos, sin)


def _attn_kernel(*refs, has_latent):
    if has_latent:
        q_ref, kc_ref, vc_ref, kl_ref, vl_ref, par_ref, o_ref = refs
    else:
        q_ref, kc_ref, vc_ref, par_ref, o_ref = refs
    q = q_ref[...]
    lane = lax.broadcasted_iota(jnp.int32, q.shape, 1)
    outs = []
    for j in range(2):
        qj = jnp.where((lane // 64) == j, q, jnp.zeros_like(q))
        sc = _dot_nt(qj, kc_ref[...])
        m = jnp.max(sc, axis=-1, keepdims=True)
        if has_latent:
            sl = _dot_nt(qj, kl_ref[...])
            m = jnp.maximum(m, jnp.max(sl, axis=-1, keepdims=True))
        pc = jnp.exp(sc - m)
        den = jnp.sum(pc, axis=-1, keepdims=True)
        acc = _dot(pc.astype(BF16), vc_ref[...])
        if has_latent:
            pl_ = jnp.exp(sl - m)
            den = den + jnp.sum(pl_, axis=-1, keepdims=True)
            acc = acc + _dot(pl_.astype(BF16), vl_ref[...])
        outs.append(acc / den)
    o = outs[0] - par_ref[0:1, :] * outs[1]
    y = o * lax.rsqrt(jnp.mean(o * o, axis=-1, keepdims=True) + EPS) * par_ref[1:2, :]
    o_ref[...] = (y * par_ref[2:3, :]).astype(o_ref.dtype)


def diff_attention(q, kc, vc, kl, vl, par, bsz, n_heads):
    t, width = q.shape
    hw = width // n_heads
    seq = t // bsz
    n_ctx = kc.shape[0] // bsz
    tq = _tile(seq, 256)
    nq = seq // tq
    has_latent = kl is not None
    in_specs = [pl.BlockSpec((tq, hw), lambda b, h, i: (b * nq + i, h)),
                pl.BlockSpec((n_ctx, hw), lambda b, h, i: (b, h)),
                pl.BlockSpec((n_ctx, hw), lambda b, h, i: (b, h))]
    args = [q, kc, vc]
    if has_latent:
        s = kl.shape[0] // bsz
        in_specs += [pl.BlockSpec((s, hw), lambda b, h, i: (b, h))] * 2
        args += [kl, vl]
    in_specs.append(pl.BlockSpec((SUBLANES, hw), lambda b, h, i: (0, 0)))
    args.append(par)
    return pl.pallas_call(
        functools.partial(_attn_kernel, has_latent=has_latent),
        grid=(bsz, n_heads, nq),
        in_specs=in_specs,
        out_specs=pl.BlockSpec((tq, hw), lambda b, h, i: (b * nq + i, h)),
        out_shape=jax.ShapeDtypeStruct((t, width), BF16),
        compiler_params=_cparams("arbitrary", "arbitrary", "arbitrary"),
        name="diff_attention",
    )(*args)


def _hgrn_kernel(ql_ref, zfl_ref, zbl_ref, vl_ref, gl_ref, qc_ref, zfc_ref, zbc_ref, vc_ref, gc_ref,
                 lb_ref, on_ref, yl_ref, yc_ref, ol_ref, oc_ref, st_ref):
    c = HG_CHUNK
    row = lax.broadcasted_iota(jnp.int32, (c, LANES), 0)
    ti = lax.broadcasted_iota(jnp.int32, (c, c), 0)
    si = lax.broadcasted_iota(jnp.int32, (c, c), 1)

    def chunk(q, z, v, d, reverse):
        lb = lb_ref[d:d + 1, :]
        f = lb + (1.0 - lb) * _sigmoid(z)
        kk = (1.0 - lb) * _sigmoid(-z)
        cum = jnp.log(jnp.maximum(f, LOGF_FLOOR))
        for s in (1, 2, 4, 8, 16, 32):
            if reverse:
                cum = cum + jnp.where(row < c - s, pltpu.roll(cum, c - s, 0), 0.0)
            else:
                cum = cum + jnp.where(row >= s, pltpu.roll(cum, s, 0), 0.0)
        mid = cum[c // 2:c // 2 + 1, :]
        c_end = cum[0:1, :] if reverse else cum[c - 1:c, :]
        qt = (q * jnp.exp(cum - mid)).astype(BF16)
        kt = (kk * jnp.exp(mid - cum)).astype(BF16)
        att = _dot_nt(qt, kt)
        att = jnp.where((ti <= si) if reverse else (ti >= si), att, 0.0)
        vb = v.astype(BF16)
        state = st_ref[d]
        o = _dot(att.astype(BF16), vb) + _dot_nt((q * jnp.exp(cum)).astype(BF16), state.astype(BF16))
        kd = (kk * jnp.exp(c_end - cum)).astype(BF16)
        st_ref[d] = state * jnp.exp(c_end) + _dot_tn(vb, kd)
        return o

    def segment(q_ref, zf_ref, zb_ref, v_ref, o_ref):
        n = q_ref.shape[0] // c

        def body(i, _):
            r0 = pl.multiple_of(i * c, c)
            o_ref[0, pl.ds(r0, c), :] = chunk(q_ref[pl.ds(r0, c), :], zf_ref[pl.ds(r0, c), :],
                                              v_ref[pl.ds(r0, c), :], 0, False)
            r1 = pl.multiple_of((n - 1 - i) * c, c)
            o_ref[1, pl.ds(r1, c), :] = chunk(q_ref[pl.ds(r1, c), :], zb_ref[pl.ds(r1, c), :],
                                              v_ref[pl.ds(r1, c), :], 1, True)
            return 0

        lax.fori_loop(0, n, body, 0)

    def finish(o_ref, g_ref, y_ref):
        o = o_ref[0] + o_ref[1]
        y = o * lax.rsqrt(jnp.mean(o * o, axis=-1, keepdims=True) + EPS) * on_ref[...]
        g = g_ref[...]
        y_ref[...] = (y * (g * _sigmoid(g))).astype(y_ref.dtype)

    st_ref[...] = jnp.zeros_like(st_ref)
    segment(qc_ref, zfc_ref, zbc_ref, vc_ref, oc_ref)
    segment(ql_ref, zfl_ref, zbl_ref, vl_ref, ol_ref)
    finish(oc_ref, gc_ref, yc_ref)
    finish(ol_ref, gl_ref, yl_ref)


def hgrn(proj_l, proj_c, bsz, lb, onorm_g):
    s = proj_l.shape[0] // bsz
    n_ctx = proj_c.shape[0] // bsz
    dv = onorm_g.shape[0]
    assert dv == LANES and s % HG_CHUNK == 0 and n_ctx % HG_CHUNK == 0
    n_heads = BRANCH_COLS // dv
    row_l = lambda col: pl.BlockSpec((s, dv), lambda b, h: (b, col * n_heads + h))
    row_c = lambda col: pl.BlockSpec((n_ctx, dv), lambda b, h: (b, col * n_heads + h))
    cols = (COL_HG_Q, COL_HG_FF, COL_HG_FB, COL_HG_I, COL_HG_G)
    return pl.pallas_call(
        _hgrn_kernel,
        grid=(bsz, n_heads),
        in_specs=[row_l(cc) for cc in cols] + [row_c(cc) for cc in cols]
        + [pl.BlockSpec((2, dv), lambda b, h: (0, h)), pl.BlockSpec((1, dv), lambda b, h: (0, 0))],
        out_specs=[pl.BlockSpec((s, dv), lambda b, h: (b, h)),
                   pl.BlockSpec((n_ctx, dv), lambda b, h: (b, h))],
        out_shape=[jax.ShapeDtypeStruct((bsz * s, BRANCH_COLS), BF16),
                   jax.ShapeDtypeStruct((bsz * n_ctx, BRANCH_COLS), BF16)],
        scratch_shapes=[pltpu.VMEM((2, s, dv), F32), pltpu.VMEM((2, n_ctx, dv), F32),
                        pltpu.VMEM((2, dv, dv), F32)],
        compiler_params=_cparams("arbitrary", "arbitrary"),
        name="hgrn",
    )(*([proj_l] * 5 + [proj_c] * 5 + [lb, onorm_g.reshape(1, dv)]))


def _merge_kernel(y0_ref, y1_ref, y2_ref, g0_ref, g1_ref, g2_ref, w_ref, bg_ref, o_ref):
    acc = None
    for kb, (y_ref, g_ref) in enumerate(((y0_ref, g0_ref), (y1_ref, g1_ref), (y2_ref, g2_ref))):
        gate = _sigmoid(g_ref[...] + bg_ref[kb:kb + 1, :])
        term = gate * _dot(y_ref[...], w_ref[kb].astype(BF16))
        acc = term if acc is None else acc + term
    o_ref[...] = acc.astype(o_ref.dtype)


def merge(ys, proj, w_branch, b_gate):
    t, k = ys[0].shape
    nb, _, d = w_branch.shape
    tm, tn = _tile(t, 1024), _tile(d, 256)
    gate_col0 = COL_GATES * BRANCH_COLS // tn
    yspec = pl.BlockSpec((tm, k), lambda i, j: (i, 0))
    gspec = lambda kb: pl.BlockSpec((tm, tn), lambda i, j: (i, gate_col0 + kb * (d // tn) + j))
    return pl.pallas_call(
        _merge_kernel,
        grid=(t // tm, d // tn),
        in_specs=[yspec, yspec, yspec, gspec(0), gspec(1), gspec(2),
                  pl.BlockSpec((nb, k, tn), lambda i, j: (0, 0, j)),
                  pl.BlockSpec((nb, tn), lambda i, j: (0, j))],
        out_specs=pl.BlockSpec((tm, tn), lambda i, j: (i, j)),
        out_shape=jax.ShapeDtypeStruct((t, d), BF16),
        compiler_params=_cparams("arbitrary", "arbitrary"),
        name="merge",
    )(ys[0], ys[1], ys[2], proj, proj, proj, w_branch, b_gate)


def _peer_tables_kernel(q_ref, sub_ref, s1_ref, e1_ref, s2_ref, e2_ref, tau_ref, top_ref):
    n_heads, n_keys, tt = s1_ref.shape
    neg_inf = jnp.float32(-jnp.inf)
    for j, s_ref in enumerate((s1_ref, s2_ref)):
        sub = sub_ref[j].astype(BF16)
        for h in range(n_heads):
            c0 = (h * 2 + j) * n_keys
            s = _dot_nt(sub, q_ref[:, c0:c0 + n_keys].astype(BF16))
            s_ref[h] = s

            def extract(i, work, j=j, h=h):
                m = jnp.max(work, axis=0, keepdims=True)
                top_ref[j, i, pl.ds(h, 1), :] = m
                return jnp.where(work == m, neg_inf, work)

            lax.fori_loop(0, PEER_TOPK, extract, s)
    a = [top_ref[0, i] for i in range(PEER_TOPK)]
    b = [top_ref[1, i] for i in range(PEER_TOPK)]
    cands = [a[i] + b[jj] for i in range(PEER_TOPK) for jj in range(PEER_TOPK) if (i + 1) * (jj + 1) <= PEER_TOPK]
    tops = []
    for _ in range(PEER_TOPK):
        m = functools.reduce(jnp.maximum, cands)
        tops.append(m)
        taken = None
        nxt = []
        for cand in cands:
            eq = cand == m
            hit = eq if taken is None else jnp.logical_and(eq, jnp.logical_not(taken))
            nxt.append(jnp.where(hit, neg_inf, cand))
            taken = eq if taken is None else jnp.logical_or(taken, eq)
        cands = nxt
    tau_ref[...] = tops[-1]
    z = functools.reduce(lambda p, q: p + q, [jnp.exp(t - tops[0]) for t in tops])
    inv_z = 1.0 / z
    for h in range(n_heads):
        e1_ref[h] = jnp.exp(s1_ref[h] - a[0][h:h + 1, :]) * inv_z[h:h + 1, :]
        e2_ref[h] = jnp.exp(s2_ref[h] - b[0][h:h + 1, :])


def peer_tables(q, subkeys, n_heads):
    t = q.shape[0]
    n_keys = subkeys.shape[1]
    assert n_keys == LANES and n_heads == SUBLANES
    tt = _tile(t, 256)
    tab = jax.ShapeDtypeStruct((n_heads, n_keys, t), F32)
    tspec = pl.BlockSpec((n_heads, n_keys, tt), lambda i: (0, 0, i))
    return pl.pallas_call(
        _peer_tables_kernel,
        grid=(t // tt,),
        in_specs=[pl.BlockSpec((tt, q.shape[1]), lambda i: (i, 0)),
                  pl.BlockSpec(subkeys.shape, lambda i: (0, 0, 0))],
        out_specs=[tspec, tspec, tspec, tspec, pl.BlockSpec((n_heads, tt), lambda i: (0, i))],
        out_shape=[tab, tab, tab, tab, jax.ShapeDtypeStruct((n_heads, t), F32)],
        scratch_shapes=[pltpu.VMEM((2, PEER_TOPK, n_heads, tt), F32)],
        compiler_params=_cparams("arbitrary"),
        name="peer_tables",
    )(q, subkeys)


def _peer_main_kernel(h_ref, u_ref, v_ref, s1_ref, e1_ref, s2_ref, e2_ref, tau_ref, res_ref, gate_ref,
                      o_ref, wa_ref):
    e = pl.program_id(1)
    groups, n_heads, _ = s1_ref.shape
    n_keys = s2_ref.shape[1]

    @pl.when(e == 0)
    def _():
        o_ref[...] = jnp.zeros_like(o_ref)

    act = _gelu(_dot_nt(u_ref[...], h_ref[...]))
    for a in range(groups):
        w = None
        for h in range(n_heads):
            total = s1_ref[a, h:h + 1, :] + s2_ref[h]
            term = jnp.where(total >= tau_ref[h:h + 1, :], e2_ref[h], 0.0) * e1_ref[a, h:h + 1, :]
            w = term if w is None else w + term
        wa_ref[a * n_keys:(a + 1) * n_keys, :] = w * act[a * n_keys:(a + 1) * n_keys, :]
    o_ref[...] += _dot(wa_ref[...].T.astype(BF16), v_ref[...])

    @pl.when(e == pl.num_programs(1) - 1)
    def _():
        o_ref[...] = res_ref[...] + gate_ref[...] * o_ref[...]


def peer_main(h, u, v, tables, res, gate, rows_per_src):
    s1, e1, s2, e2, tau = tables
    s1, e1 = jnp.swapaxes(s1, 0, 1), jnp.swapaxes(e1, 0, 1)
    t, d = h.shape
    n_exp = u.shape[0]
    n_heads, n_keys, _ = s2.shape
    tm = _tile(rows_per_src, 512)
    groups = 4
    te = groups * n_keys
    tok = pl.BlockSpec((tm, d), lambda i, e: (i, 0))
    exp = pl.BlockSpec((te, d), lambda i, e: (e, 0))
    grp = pl.BlockSpec((groups, n_heads, tm), lambda i, e: (e, 0, i))
    full = pl.BlockSpec((n_heads, n_keys, tm), lambda i, e: (0, 0, i))
    return pl.pallas_call(
        _peer_main_kernel,
        grid=(t // tm, n_exp // te),
        in_specs=[tok, exp, exp, grp, grp, full, full,
                  pl.BlockSpec((n_heads, tm), lambda i, e: (0, i)),
                  tok,
                  pl.BlockSpec((None, 1, d), lambda i, e: (i * tm // rows_per_src, 0, 0))],
        out_specs=tok,
        out_shape=jax.ShapeDtypeStruct((t, d), F32),
        scratch_shapes=[pltpu.VMEM((te, tm), F32)],
        compiler_params=_cparams("arbitrary", "arbitrary"),
        name="peer_main",
    )(h, u, v, s1, e1, s2, e2, tau, res, gate)


def _rope_tables(seq, head_dim):
    rows = seq // GRID_W
    axis_dim = head_dim // 2
    r = jnp.repeat(jnp.arange(rows, dtype=F32), GRID_W)
    col = jnp.tile(jnp.arange(GRID_W, dtype=F32), rows)
    inv = ROPE_BASE ** (-jnp.arange(0, axis_dim, 2, dtype=F32) / axis_dim)
    ang = jnp.concatenate([r[:, None] * inv, col[:, None] * inv], axis=-1)
    cos, sin = jnp.cos(ang), jnp.sin(ang)
    reps = LANES // head_dim
    cos_l = jnp.tile(jnp.concatenate([cos, cos], axis=-1), (1, reps))
    sin_l = jnp.tile(jnp.concatenate([-sin, sin], axis=-1), (1, reps))
    return cos_l, sin_l


def kernel(x, c, ctx, c_ctx, w_ada, b_ada, norm_mix_g, norm_ffn_g, w_in, rg_conv_w, rg_conv_b, rg_wa, rg_ba, rg_wx, rg_bx, rg_lambda, da_lq, da_lk, da_subln_g, hg_lb, hg_onorm_g, w_branch, b_gate, w_out, peer_wq, peer_subkeys, peer_u, peer_v, final_norm_g):
    bsz, s, d = x.shape
    n_ctx = ctx.shape[1]
    depth = w_in.shape[0]
    head_dim = da_lq.shape[-1]
    da_heads = BRANCH_COLS // (2 * head_dim)
    peer_heads = peer_wq.shape[-1] // (2 * peer_subkeys.shape[-1])
    assert bsz + 1 <= SUBLANES and 2 * head_dim == LANES

    xl = x.reshape(bsz * s, d)
    xc = ctx.reshape(bsz * n_ctx, d)
    cvec = jnp.concatenate([c, c_ctx[None, :], jnp.zeros((SUBLANES - bsz - 1, d), F32)], axis=0)
    mods = ada_mods(cvec, w_ada, b_ada).reshape(depth, SUBLANES, N_MOD, d)
    lb = jnp.cumsum(jax.nn.softmax(hg_lb.astype(F32), axis=1), axis=1)
    lb = lb - lb[:, :1]
    cos_l, sin_l = _rope_tables(s, head_dim)
    scale = head_dim ** -0.5

    for l in range(depth):
        last = l == depth - 1
        ml = [mods[l, :bsz, k][:, None, :] for k in range(N_MOD)]
        mc = [mods[l, bsz:bsz + 1, k][:, None, :] for k in range(N_MOD)]
        lam_init = 0.8 - 0.6 * math.exp(-0.3 * l)
        lam = jnp.exp(jnp.sum(da_lq[l, 0] * da_lk[l, 0])) - jnp.exp(jnp.sum(da_lq[l, 1] * da_lk[l, 1])) + lam_init
        par = jnp.concatenate([jnp.full((1, LANES), lam, F32), da_subln_g[l][None, :],
                               jnp.full((1, LANES), 1.0 - lam_init, F32),
                               jnp.zeros((SUBLANES - 3, LANES), F32)], axis=0)

        hl = norm_mod(xl, norm_mix_g[l], ml[0], ml[1], s)
        hc = norm_mod(xc, norm_mix_g[l], mc[0], mc[1], bsz * n_ctx)
        pj_l = matmul(hl, w_in[l], F32)
        pj_c = matmul(hc, w_in[l], F32)
        rg_l, rg_c = rglru(pj_l, pj_c, bsz, rg_conv_w[l], rg_conv_b[l], rg_wa[l], rg_ba[l], rg_wx[l], rg_bx[l],
                           rg_lambda[l])
        ql, kl, vl = qkv_prep(pj_l, cos_l, sin_l, s, scale, True)
        qc, kc, vc = qkv_prep(pj_c, cos_l, sin_l, n_ctx, scale, False)
        da_l = diff_attention(ql, kc, vc, kl, vl, par, bsz, da_heads)
        hg_l, hg_c = hgrn(pj_l, pj_c, bsz, lb[:, l], hg_onorm_g[l])
        mg_l = merge((rg_l, da_l, hg_l), pj_l, w_branch[l], b_gate[l])
        xl = matmul_residual(mg_l, w_out[l], xl, ml[2], s)
        if not last:
            da_c = diff_attention(qc, kc, vc, None, None, par, bsz, da_heads)
            mg_c = merge((rg_c, da_c, hg_c), pj_c, w_branch[l], b_gate[l])
            xc = matmul_residual(mg_c, w_out[l], xc, mc[2], bsz * n_ctx)

        u_bf = peer_u[l].astype(BF16)
        v_bf = peer_v[l].astype(BF16)
        hl = norm_mod(xl, norm_ffn_g[l], ml[3], ml[4], s)
        tabs = peer_tables(matmul(hl, peer_wq[l], F32), peer_subkeys[l], peer_heads)
        xl = peer_main(hl, u_bf, v_bf, tabs, xl, ml[5], s)
        if not last:
            hc = norm_mod(xc, norm_ffn_g[l], mc[3], mc[4], bsz * n_ctx)
            tabs = peer_tables(matmul(hc, peer_wq[l], F32), peer_subkeys[l], peer_heads)
            xc = peer_main(hc, u_bf, v_bf, tabs, xc, mc[5], bsz * n_ctx)

    return plain_norm(xl, final_norm_g).reshape(bsz, s, d)
```

```python
import functools
import math

import jax
import jax.numpy as jnp
from jax import lax
from jax.experimental import pallas as pl
from jax.experimental.pallas import tpu as pltpu

F32 = jnp.float32
BF16 = jnp.bfloat16

N_MOD = 6
EPS = 1e-6
GRID_W = 64
ROPE_BASE = 10000.0
RG_C = 8.0
LOGF_FLOOR = 1e-20
HG_CHUNK = 64
PEER_TOPK = 16
BRANCH_COLS = 1024
LANES = 128
SUBLANES = 8
VMEM_LIMIT_BYTES = 56 * 1024 * 1024

COL_RG_X, COL_RG_GATE, COL_DA_Q, COL_DA_K, COL_DA_V, COL_HG_Q, COL_HG_FF, COL_HG_FB, COL_HG_I, COL_HG_G, COL_GATES = range(11)


def _cparams(*sem):
    return pltpu.CompilerParams(dimension_semantics=sem, vmem_limit_bytes=VMEM_LIMIT_BYTES)


def _tile(n, pref):
    t = min(n, pref)
    assert n % t == 0, (n, pref)
    return t


def _gelu(x):
    return 0.5 * x * (1.0 + jnp.tanh(math.sqrt(2.0 / math.pi) * (x + 0.044715 * (x * x * x))))


def _sigmoid(x):
    return 1.0 / (1.0 + jnp.exp(-x))


def _dot(a, b):
    return jnp.dot(a, b, preferred_element_type=F32)


def _dot_nt(a, b):
    return lax.dot_general(a, b, (((1,), (1,)), ((), ())), preferred_element_type=F32)


def _dot_tn(a, b):
    return lax.dot_general(a, b, (((0,), (0,)), ((), ())), preferred_element_type=F32)


def _ada_kernel(c_ref, w_ref, b_ref, o_ref):
    c = c_ref[...]
    sc = c * _sigmoid(c)
    o_ref[...] = _dot(sc.astype(BF16), w_ref[...].astype(BF16)) + b_ref[...]


def ada_mods(cvec, w_ada, b_ada):
    depth, d, n = w_ada.shape
    tn = _tile(n, 1024)
    return pl.pallas_call(
        _ada_kernel,
        grid=(depth, n // tn),
        in_specs=[pl.BlockSpec((SUBLANES, d), lambda l, j: (0, 0)),
                  pl.BlockSpec((None, d, tn), lambda l, j: (l, 0, j)),
                  pl.BlockSpec((None, 1, tn), lambda l, j: (l, 0, j))],
        out_specs=pl.BlockSpec((None, SUBLANES, tn), lambda l, j: (l, 0, j)),
        out_shape=jax.ShapeDtypeStruct((depth, SUBLANES, n), F32),
        compiler_params=_cparams("arbitrary", "arbitrary"),
        name="ada_mods",
    )(cvec, w_ada, b_ada.reshape(depth, 1, n))


def _norm_kernel(x_ref, g_ref, sh_ref, sc_ref, o_ref):
    x = x_ref[...]
    y = x * lax.rsqrt(jnp.mean(x * x, axis=-1, keepdims=True) + EPS) * g_ref[...]
    o_ref[...] = (y * (1.0 + sc_ref[...]) + sh_ref[...]).astype(o_ref.dtype)


def _plain_norm_kernel(x_ref, g_ref, o_ref):
    x = x_ref[...]
    o_ref[...] = (x * lax.rsqrt(jnp.mean(x * x, axis=-1, keepdims=True) + EPS) * g_ref[...]).astype(o_ref.dtype)


def norm_mod(x, g, shift, scale, rows_per_src):
    t, d = x.shape
    tr = _tile(rows_per_src, 512)
    src = lambda i: (i * tr // rows_per_src, 0, 0)
    return pl.pallas_call(
        _norm_kernel,
        grid=(t // tr,),
        in_specs=[pl.BlockSpec((tr, d), lambda i: (i, 0)),
                  pl.BlockSpec((1, d), lambda i: (0, 0)),
                  pl.BlockSpec((None, 1, d), src),
                  pl.BlockSpec((None, 1, d), src)],
        out_specs=pl.BlockSpec((tr, d), lambda i: (i, 0)),
        out_shape=jax.ShapeDtypeStruct((t, d), BF16),
        compiler_params=_cparams("arbitrary"),
        name="norm_mod",
    )(x, g.reshape(1, d), shift, scale)


def plain_norm(x, g):
    t, d = x.shape
    tr = _tile(t, 512)
    return pl.pallas_call(
        _plain_norm_kernel,
        grid=(t // tr,),
        in_specs=[pl.BlockSpec((tr, d), lambda i: (i, 0)),
                  pl.BlockSpec((1, d), lambda i: (0, 0))],
        out_specs=pl.BlockSpec((tr, d), lambda i: (i, 0)),
        out_shape=jax.ShapeDtypeStruct((t, d), x.dtype),
        compiler_params=_cparams("arbitrary"),
        name="final_norm",
    )(x, g.reshape(1, d))


def _mm_kernel(a_ref, w_ref, o_ref):
    o_ref[...] = _dot(a_ref[...], w_ref[...].astype(BF16)).astype(o_ref.dtype)


def matmul(a, w, layer, out_dtype, tm_pref=1024, tn_pref=512):
    m, k = a.shape
    n = w.shape[2]
    tm, tn = _tile(m, tm_pref), _tile(n, tn_pref)
    return pl.pallas_call(
        _mm_kernel,
        grid=(m // tm, n // tn),
        in_specs=[pl.BlockSpec((tm, k), lambda i, j: (i, 0)),
                  pl.BlockSpec((None, k, tn), lambda i, j: (layer, 0, j))],
        out_specs=pl.BlockSpec((tm, tn), lambda i, j: (i, j)),
        out_shape=jax.ShapeDtypeStruct((m, n), out_dtype),
        compiler_params=_cparams("arbitrary", "arbitrary"),
        name="matmul",
    )(a, w)


def _mm_res_kernel(a_ref, w_ref, res_ref, gate_ref, o_ref):
    o_ref[...] = res_ref[...] + gate_ref[...] * _dot(a_ref[...], w_ref[...].astype(BF16))


def matmul_residual(a, w, layer, res, gate, rows_per_src):
    m, k = a.shape
    n = w.shape[2]
    tm, tn = _tile(rows_per_src, 1024), _tile(n, 512)
    return pl.pallas_call(
        _mm_res_kernel,
        grid=(m // tm, n // tn),
        in_specs=[pl.BlockSpec((tm, k), lambda i, j: (i, 0)),
                  pl.BlockSpec((None, k, tn), lambda i, j: (layer, 0, j)),
                  pl.BlockSpec((tm, tn), lambda i, j: (i, j)),
                  pl.BlockSpec((None, 1, tn), lambda i, j: (i * tm // rows_per_src, 0, j))],
        out_specs=pl.BlockSpec((tm, tn), lambda i, j: (i, j)),
        out_shape=jax.ShapeDtypeStruct((m, n), F32),
        compiler_params=_cparams("arbitrary", "arbitrary"),
        name="matmul_residual",
    )(a, w, res, gate)


def _rglru_kernel(xl_ref, gl_ref, xc_ref, gc_ref, cw_ref, cb_ref, wa_ref, ba_ref, wx_ref, bx_ref, lam_ref,
                  yl_ref, yc_ref, al_ref, bl_ref, ac_ref, bc_ref):
    cw = cw_ref[...]
    cb = cb_ref[...]

    def coeffs(x_ref, a_ref, b_ref):
        n = x_ref.shape[0]
        x = x_ref[...]
        row = lax.broadcasted_iota(jnp.int32, x.shape, 0)
        xm2 = jnp.where(row >= 2, pltpu.roll(x, 2, 0), 0.0)
        xm1 = jnp.where(row >= 1, pltpu.roll(x, 1, 0), 0.0)
        xp1 = jnp.where(row < n - 1, pltpu.roll(x, n - 1, 0), 0.0)
        u = cb + cw[0:1] * xm2
        u = u + cw[1:2] * xm1
        u = u + cw[2:3] * x
        u = u + cw[3:4] * xp1
        ub = u.astype(BF16)
        for d in range(2):
            r = _sigmoid(_dot(ub, wa_ref[d].astype(BF16)) + ba_ref[d:d + 1, :])
            i = _sigmoid(_dot(ub, wx_ref[d].astype(BF16)) + bx_ref[d:d + 1, :])
            lam = lam_ref[d:d + 1, :]
            softplus_neg_lam = jnp.maximum(-lam, 0.0) + jnp.log(1.0 + jnp.exp(-jnp.abs(lam)))
            log_a = -RG_C * r * softplus_neg_lam
            a_ref[d] = jnp.exp(log_a)
            th = jnp.tanh(log_a)
            b_ref[d] = jnp.sqrt(-2.0 * th / (1.0 - th)) * (i * u)

    row8 = lax.broadcasted_iota(jnp.int32, (SUBLANES, LANES), 0)

    def scan(a_ref, b_ref, carry):
        nblk = a_ref.shape[1] // SUBLANES

        def body(i, carry):
            cf, cbk = carry
            r0 = pl.multiple_of(i * SUBLANES, SUBLANES)
            a = a_ref[0, pl.ds(r0, SUBLANES), :]
            b = b_ref[0, pl.ds(r0, SUBLANES), :]
            for s in (1, 2, 4):
                m = row8 >= s
                b = jnp.where(m, a * pltpu.roll(b, s, 0) + b, b)
                a = jnp.where(m, a * pltpu.roll(a, s, 0), a)
            h = b + a * cf
            b_ref[0, pl.ds(r0, SUBLANES), :] = h
            cf = h[SUBLANES - 1:SUBLANES, :]
            r1 = pl.multiple_of((nblk - 1 - i) * SUBLANES, SUBLANES)
            a = a_ref[1, pl.ds(r1, SUBLANES), :]
            b = b_ref[1, pl.ds(r1, SUBLANES), :]
            for s in (1, 2, 4):
                m = row8 < SUBLANES - s
                b = jnp.where(m, a * pltpu.roll(b, SUBLANES - s, 0) + b, b)
                a = jnp.where(m, a * pltpu.roll(a, SUBLANES - s, 0), a)
            h = b + a * cbk
            b_ref[1, pl.ds(r1, SUBLANES), :] = h
            cbk = h[0:1, :]
            return cf, cbk

        return lax.fori_loop(0, nblk, body, carry)

    coeffs(xc_ref, ac_ref, bc_ref)
    coeffs(xl_ref, al_ref, bl_ref)
    zero = jnp.zeros((1, LANES), F32)
    carry = scan(ac_ref, bc_ref, (zero, zero))
    scan(al_ref, bl_ref, carry)
    yc_ref[...] = ((bc_ref[0] + bc_ref[1]) * _gelu(gc_ref[...])).astype(yc_ref.dtype)
    yl_ref[...] = ((bl_ref[0] + bl_ref[1]) * _gelu(gl_ref[...])).astype(yl_ref.dtype)


def rglru(proj_l, proj_c, bsz, conv_w, conv_b, wa, ba, wx, bx, lam):
    s = proj_l.shape[0] // bsz
    n_ctx = proj_c.shape[0] // bsz
    nblk, bd = wa.shape[1], wa.shape[2]
    assert bd == LANES
    width = nblk * bd
    cpb = BRANCH_COLS // bd
    row_l = lambda col: pl.BlockSpec((s, bd), lambda b, j: (b, col * cpb + j))
    row_c = lambda col: pl.BlockSpec((n_ctx, bd), lambda b, j: (b, col * cpb + j))
    vec = lambda r: pl.BlockSpec((r, bd), lambda b, j: (0, j))
    wspec = pl.BlockSpec((2, None, bd, bd), lambda b, j: (0, j, 0, 0))
    return pl.pallas_call(
        _rglru_kernel,
        grid=(bsz, nblk),
        in_specs=[row_l(COL_RG_X), row_l(COL_RG_GATE), row_c(COL_RG_X), row_c(COL_RG_GATE),
                  vec(conv_w.shape[0]), vec(1), wspec, vec(2), wspec, vec(2), vec(2)],
        out_specs=[pl.BlockSpec((s, bd), lambda b, j: (b, j)),
                   pl.BlockSpec((n_ctx, bd), lambda b, j: (b, j))],
        out_shape=[jax.ShapeDtypeStruct((bsz * s, width), BF16),
                   jax.ShapeDtypeStruct((bsz * n_ctx, width), BF16)],
        scratch_shapes=[pltpu.VMEM((2, s, bd), F32), pltpu.VMEM((2, s, bd), F32),
                        pltpu.VMEM((2, n_ctx, bd), F32), pltpu.VMEM((2, n_ctx, bd), F32)],
        compiler_params=_cparams("arbitrary", "arbitrary"),
        name="rglru",
    )(proj_l, proj_l, proj_c, proj_c, conv_w, conv_b.reshape(1, width), wa, ba, wx, bx, lam)


def _qkv_prep_kernel(q_ref, k_ref, v_ref, cos_ref, sin_ref, qo_ref, ko_ref, vo_ref, *, scale, rope):
    tr = q_ref.shape[0]
    lane = lax.broadcasted_iota(jnp.int32, (tr, LANES), 1)
    first_half = (lane % 64) < 32
    cos = cos_ref[...]
    sin = sin_ref[...]
    for g in range(q_ref.shape[1] // LANES):
        sl = slice(g * LANES, (g + 1) * LANES)
        for src, dst, mult in ((q_ref, qo_ref, scale), (k_ref, ko_ref, None)):
            x = src[:, sl]
            if rope:
                partner = jnp.where(first_half, pltpu.roll(x, LANES - 32, 1), pltpu.roll(x, 32, 1))
                x = x * cos + partner * sin
            if mult is not None:
                x = x * mult
            dst[:, sl] = x.astype(dst.dtype)
    vo_ref[...] = v_ref[...].astype(vo_ref.dtype)


def qkv_prep(proj, cos, sin, seq, scale, rope):
    t = proj.shape[0]
    tr = _tile(seq, 512)
    col = lambda c: pl.BlockSpec((tr, BRANCH_COLS), lambda i: (i, c))
    tab = pl.BlockSpec((tr, LANES), lambda i: (i % (seq // tr), 0))
    out = jax.ShapeDtypeStruct((t, BRANCH_COLS), BF16)
    return pl.pallas_call(
        functools.partial(_qkv_prep_kernel, scale=scale, rope=rope),
        grid=(t // tr,),
        in_specs=[col(COL_DA_Q), col(COL_DA_K), col(COL_DA_V), tab, tab],
        out_specs=[pl.BlockSpec((tr, BRANCH_COLS), lambda i: (i, 0))] * 3,
        out_shape=[out, out, out],
        compiler_params=_cparams("arbitrary"),
        name="qkv_prep",
    )(proj, proj, proj, cos, sin)


def _attn_kernel(*refs, has_latent):
    if has_latent:
        q_ref, kc_ref, vc_ref, kl_ref, vl_ref, par_ref, o_ref = refs
    else:
        q_ref, kc_ref, vc_ref, par_ref, o_ref = refs
    q = q_ref[...]
    lane = lax.broadcasted_iota(jnp.int32, q.shape, 1)
    outs = []
    for j in range(2):
        qj = jnp.where((lane // 64) == j, q, jnp.zeros_like(q))
        sc = _dot_nt(qj, kc_ref[...])
        m = jnp.max(sc, axis=-1, keepdims=True)
        if has_latent:
            sl = _dot_nt(qj, kl_ref[...])
            m = jnp.maximum(m, jnp.max(sl, axis=-1, keepdims=True))
        pc = jnp.exp(sc - m)
        den = jnp.sum(pc, axis=-1, keepdims=True)
        acc = _dot(pc.astype(BF16), vc_ref[...])
        if has_latent:
            pl_ = jnp.exp(sl - m)
            den = den + jnp.sum(pl_, axis=-1, keepdims=True)
            acc = acc + _dot(pl_.astype(BF16), vl_ref[...])
        outs.append(acc / den)
    o = outs[0] - par_ref[0:1, :] * outs[1]
    y = o * lax.rsqrt(jnp.mean(o * o, axis=-1, keepdims=True) + EPS) * par_ref[1:2, :]
    o_ref[...] = (y * par_ref[2:3, :]).astype(o_ref.dtype)


def diff_attention(q, kc, vc, kl, vl, par, bsz, n_heads):
    t, width = q.shape
    hw = width // n_heads
    seq = t // bsz
    n_ctx = kc.shape[0] // bsz
    tq = _tile(seq, 256)
    nq = seq // tq
    has_latent = kl is not None
    in_specs = [pl.BlockSpec((tq, hw), lambda b, h, i: (b * nq + i, h)),
                pl.BlockSpec((n_ctx, hw), lambda b, h, i: (b, h)),
                pl.BlockSpec((n_ctx, hw), lambda b, h, i: (b, h))]
    args = [q, kc, vc]
    if has_latent:
        s = kl.shape[0] // bsz
        in_specs += [pl.BlockSpec((s, hw), lambda b, h, i: (b, h))] * 2
        args += [kl, vl]
    in_specs.append(pl.BlockSpec((SUBLANES, hw), lambda b, h, i: (0, 0)))
    args.append(par)
    return pl.pallas_call(
        functools.partial(_attn_kernel, has_latent=has_latent),
        grid=(bsz, n_heads, nq),
        in_specs=in_specs,
        out_specs=pl.BlockSpec((tq, hw), lambda b, h, i: (b * nq + i, h)),
        out_shape=jax.ShapeDtypeStruct((t, width), BF16),
        compiler_params=_cparams("arbitrary", "arbitrary", "arbitrary"),
        name="diff_attention",
    )(*args)


def _hgrn_kernel(ql_ref, zfl_ref, zbl_ref, vl_ref, gl_ref, qc_ref, zfc_ref, zbc_ref, vc_ref, gc_ref,
                 lb_ref, on_ref, yl_ref, yc_ref, ol_ref, oc_ref, st_ref):
    c = HG_CHUNK
    row = lax.broadcasted_iota(jnp.int32, (c, LANES), 0)
    ti = lax.broadcasted_iota(jnp.int32, (c, c), 0)
    si = lax.broadcasted_iota(jnp.int32, (c, c), 1)

    def chunk(q, z, v, d, reverse):
        lb = lb_ref[d:d + 1, :]
        f = lb + (1.0 - lb) * _sigmoid(z)
        kk = (1.0 - lb) * _sigmoid(-z)
        cum = jnp.log(jnp.maximum(f, LOGF_FLOOR))
        for s in (1, 2, 4, 8, 16, 32):
            if reverse:
                cum = cum + jnp.where(row < c - s, pltpu.roll(cum, c - s, 0), 0.0)
            else:
                cum = cum + jnp.where(row >= s, pltpu.roll(cum, s, 0), 0.0)
        mid = cum[c // 2:c // 2 + 1, :]
        c_end = cum[0:1, :] if reverse else cum[c - 1:c, :]
        qt = (q * jnp.exp(cum - mid)).astype(BF16)
        kt = (kk * jnp.exp(mid - cum)).astype(BF16)
        att = _dot_nt(qt, kt)
        att = jnp.where((ti <= si) if reverse else (ti >= si), att, 0.0)
        vb = v.astype(BF16)
        state = st_ref[d]
        o = _dot(att.astype(BF16), vb) + _dot_nt((q * jnp.exp(cum)).astype(BF16), state.astype(BF16))
        kd = (kk * jnp.exp(c_end - cum)).astype(BF16)
        st_ref[d] = state * jnp.exp(c_end) + _dot_tn(vb, kd)
        return o

    def segment(q_ref, zf_ref, zb_ref, v_ref, o_ref):
        n = q_ref.shape[0] // c

        def body(i, _):
            r0 = pl.multiple_of(i * c, c)
            o_ref[0, pl.ds(r0, c), :] = chunk(q_ref[pl.ds(r0, c), :], zf_ref[pl.ds(r0, c), :],
                                              v_ref[pl.ds(r0, c), :], 0, False)
            r1 = pl.multiple_of((n - 1 - i) * c, c)
            o_ref[1, pl.ds(r1, c), :] = chunk(q_ref[pl.ds(r1, c), :], zb_ref[pl.ds(r1, c), :],
                                              v_ref[pl.ds(r1, c), :], 1, True)
            return 0

        lax.fori_loop(0, n, body, 0)

    def finish(o_ref, g_ref, y_ref):
        o = o_ref[0] + o_ref[1]
        y = o * lax.rsqrt(jnp.mean(o * o, axis=-1, keepdims=True) + EPS) * on_ref[...]
        g = g_ref[...]
        y_ref[...] = (y * (g * _sigmoid(g))).astype(y_ref.dtype)

    st_ref[...] = jnp.zeros_like(st_ref)
    segment(qc_ref, zfc_ref, zbc_ref, vc_ref, oc_ref)
    segment(ql_ref, zfl_ref, zbl_ref, vl_ref, ol_ref)
    finish(oc_ref, gc_ref, yc_ref)
    finish(ol_ref, gl_ref, yl_ref)


def hgrn(proj_l, proj_c, bsz, lb, onorm_g):
    s = proj_l.shape[0] // bsz
    n_ctx = proj_c.shape[0] // bsz
    dv = onorm_g.shape[0]
    assert dv == LANES and s % HG_CHUNK == 0 and n_ctx % HG_CHUNK == 0
    n_heads = BRANCH_COLS // dv
    row_l = lambda col: pl.BlockSpec((s, dv), lambda b, h: (b, col * n_heads + h))
    row_c = lambda col: pl.BlockSpec((n_ctx, dv), lambda b, h: (b, col * n_heads + h))
    cols = (COL_HG_Q, COL_HG_FF, COL_HG_FB, COL_HG_I, COL_HG_G)
    return pl.pallas_call(
        _hgrn_kernel,
        grid=(bsz, n_heads),
        in_specs=[row_l(cc) for cc in cols] + [row_c(cc) for cc in cols]
        + [pl.BlockSpec((2, dv), lambda b, h: (0, h)), pl.BlockSpec((1, dv), lambda b, h: (0, 0))],
        out_specs=[pl.BlockSpec((s, dv), lambda b, h: (b, h)),
                   pl.BlockSpec((n_ctx, dv), lambda b, h: (b, h))],
        out_shape=[jax.ShapeDtypeStruct((bsz * s, BRANCH_COLS), BF16),
                   jax.ShapeDtypeStruct((bsz * n_ctx, BRANCH_COLS), BF16)],
        scratch_shapes=[pltpu.VMEM((2, s, dv), F32), pltpu.VMEM((2, n_ctx, dv), F32),
                        pltpu.VMEM((2, dv, dv), F32)],
        compiler_params=_cparams("arbitrary", "arbitrary"),
        name="hgrn",
    )(*([proj_l] * 5 + [proj_c] * 5 + [lb, onorm_g.reshape(1, dv)]))


def _merge_kernel(y0_ref, y1_ref, y2_ref, g0_ref, g1_ref, g2_ref, w_ref, bg_ref, o_ref):
    acc = None
    for kb, (y_ref, g_ref) in enumerate(((y0_ref, g0_ref), (y1_ref, g1_ref), (y2_ref, g2_ref))):
        gate = _sigmoid(g_ref[...] + bg_ref[kb:kb + 1, :])
        term = gate * _dot(y_ref[...], w_ref[kb].astype(BF16))
        acc = term if acc is None else acc + term
    o_ref[...] = acc.astype(o_ref.dtype)


def merge(ys, proj, w_branch, layer, b_gate):
    t, k = ys[0].shape
    _, nb, _, d = w_branch.shape
    tm, tn = _tile(t, 1024), _tile(d, 256)
    gate_col0 = COL_GATES * BRANCH_COLS // tn
    yspec = pl.BlockSpec((tm, k), lambda i, j: (i, 0))
    gspec = lambda kb: pl.BlockSpec((tm, tn), lambda i, j: (i, gate_col0 + kb * (d // tn) + j))
    return pl.pallas_call(
        _merge_kernel,
        grid=(t // tm, d // tn),
        in_specs=[yspec, yspec, yspec, gspec(0), gspec(1), gspec(2),
                  pl.BlockSpec((None, nb, k, tn), lambda i, j: (layer, 0, 0, j)),
                  pl.BlockSpec((nb, tn), lambda i, j: (0, j))],
        out_specs=pl.BlockSpec((tm, tn), lambda i, j: (i, j)),
        out_shape=jax.ShapeDtypeStruct((t, d), BF16),
        compiler_params=_cparams("arbitrary", "arbitrary"),
        name="merge",
    )(ys[0], ys[1], ys[2], proj, proj, proj, w_branch, b_gate)


def _peer_tables_kernel(q_ref, sub_ref, c1_ref, e1_ref, r2_ref, e2_ref, top_ref, s_ref, rank_ref):
    _, n_heads, n_keys, tt = s_ref.shape
    neg_inf = jnp.float32(-jnp.inf)
    for j in range(2):
        sub = sub_ref[j].astype(BF16)
        for h in range(n_heads):
            c0 = (h * 2 + j) * n_keys
            s = _dot_nt(sub, q_ref[:, c0:c0 + n_keys].astype(BF16))
            s_ref[j, h] = s

            def extract(i, carry, j=j, h=h):
                work, rank = carry
                m = jnp.max(work, axis=0, keepdims=True)
                top_ref[j, i, pl.ds(h, 1), :] = m
                eq = work == m
                return jnp.where(eq, neg_inf, work), jnp.where(eq, i.astype(F32), rank)

            _, rank = lax.fori_loop(0, PEER_TOPK, extract, (s, jnp.full(s.shape, float(PEER_TOPK), F32)))
            rank_ref[j, h] = rank
    a = [top_ref[0, i] for i in range(PEER_TOPK)]
    b = [top_ref[1, i] for i in range(PEER_TOPK)]
    cands = [a[i] + b[jj] for i in range(PEER_TOPK) for jj in range(PEER_TOPK) if (i + 1) * (jj + 1) <= PEER_TOPK]
    tops = []
    for _ in range(PEER_TOPK):
        m = functools.reduce(jnp.maximum, cands)
        tops.append(m)
        taken = None
        nxt = []
        for cand in cands:
            eq = cand == m
            hit = eq if taken is None else jnp.logical_and(eq, jnp.logical_not(taken))
            nxt.append(jnp.where(hit, neg_inf, cand))
            taken = eq if taken is None else jnp.logical_or(taken, eq)
        cands = nxt
    tau = tops[-1]
    z = functools.reduce(lambda p, q: p + q, [jnp.exp(t - tops[0]) for t in tops])
    inv_z = 1.0 / z
    count = []
    for i in range(PEER_TOPK):
        n = None
        for jj in range(PEER_TOPK // (i + 1)):
            hit = jnp.where(a[i] + b[jj] >= tau, 1.0, 0.0)
            n = hit if n is None else n + hit
        count.append(n)
    for h in range(n_heads):
        rank1 = rank_ref[0, h]
        c1 = jnp.zeros_like(rank1)
        for i in range(PEER_TOPK):
            c1 = jnp.where(rank1 == float(i), count[i][h:h + 1, :], c1)
        c1_ref[h] = c1
        r2_ref[h] = rank_ref[1, h].astype(r2_ref.dtype)
        e1_ref[h] = jnp.exp(s_ref[0, h] - a[0][h:h + 1, :]) * inv_z[h:h + 1, :]
        e2_ref[h] = jnp.exp(s_ref[1, h] - b[0][h:h + 1, :]).astype(e2_ref.dtype)


def peer_tables(q, subkeys, n_heads):
    t = q.shape[0]
    n_keys = subkeys.shape[1]
    assert n_keys == LANES and n_heads == SUBLANES
    tt = _tile(t, 256)
    tspec = pl.BlockSpec((n_heads, n_keys, tt), lambda i: (0, 0, i))
    tab = lambda dt: jax.ShapeDtypeStruct((n_heads, n_keys, t), dt)
    return pl.pallas_call(
        _peer_tables_kernel,
        grid=(t // tt,),
        in_specs=[pl.BlockSpec((tt, q.shape[1]), lambda i: (i, 0)),
                  pl.BlockSpec(subkeys.shape, lambda i: (0, 0, 0))],
        out_specs=[tspec, tspec, tspec, tspec],
        out_shape=[tab(F32), tab(F32), tab(BF16), tab(BF16)],
        scratch_shapes=[pltpu.VMEM((2, PEER_TOPK, n_heads, tt), F32),
                        pltpu.VMEM((2, n_heads, n_keys, tt), F32),
                        pltpu.VMEM((2, n_heads, n_keys, tt), F32)],
        compiler_params=_cparams("arbitrary"),
        name="peer_tables",
    )(q, subkeys)


def _peer_main_kernel(h_ref, u_ref, v_ref, c1_ref, e1_ref, r2_ref, e2_ref, res_ref, gate_ref, o_ref, w_ref):
    e = pl.program_id(1)
    groups, n_heads, tm = c1_ref.shape
    n_keys = r2_ref.shape[1]

    @pl.when(e == 0)
    def _():
        o_ref[...] = jnp.zeros_like(o_ref)

    act = _gelu(_dot_nt(u_ref[...], h_ref[...]))
    for c in range(tm // LANES):
        cs = slice(c * LANES, (c + 1) * LANES)
        for a0 in range(0, groups, 2):
            accs = [None, None]
            for h in range(n_heads):
                r2c = r2_ref[h, :, cs]
                e2c = e2_ref[h, :, cs]
                for k in range(2):
                    c1row = c1_ref[a0 + k, h:h + 1, cs].astype(BF16)
                    e1row = e1_ref[a0 + k, h:h + 1, cs].astype(BF16)
                    term = jnp.where(r2c < c1row, e2c, jnp.zeros_like(e2c)) * e1row
                    accs[k] = term if accs[k] is None else accs[k] + term
            for k in range(2):
                w_ref[(a0 + k) * n_keys:(a0 + k + 1) * n_keys, cs] = accs[k].astype(F32)
    o_ref[...] += _dot((w_ref[...] * act).T.astype(BF16), v_ref[...])

    @pl.when(e == pl.num_programs(1) - 1)
    def _():
        o_ref[...] = res_ref[...] + gate_ref[...] * o_ref[...]


def peer_main(h, u, v, layer, tables, res, gate, rows_per_src):
    c1, e1, r2, e2 = tables
    c1, e1 = jnp.swapaxes(c1, 0, 1), jnp.swapaxes(e1, 0, 1)
    t, d = h.shape
    n_exp = u.shape[1]
    n_heads, n_keys, _ = r2.shape
    tm = _tile(rows_per_src, 512)
    groups = 8
    te = groups * n_keys
    tok = pl.BlockSpec((tm, d), lambda i, e: (i, 0))
    exp = pl.BlockSpec((None, te, d), lambda i, e: (layer, e, 0))
    grp = pl.BlockSpec((groups, n_heads, tm), lambda i, e: (e, 0, i))
    full = pl.BlockSpec((n_heads, n_keys, tm), lambda i, e: (0, 0, i))
    return pl.pallas_call(
        _peer_main_kernel,
        grid=(t // tm, n_exp // te),
        in_specs=[tok, exp, exp, grp, grp, full, full, tok,
                  pl.BlockSpec((None, 1, d), lambda i, e: (i * tm // rows_per_src, 0, 0))],
        out_specs=tok,
        out_shape=jax.ShapeDtypeStruct((t, d), F32),
        scratch_shapes=[pltpu.VMEM((te, tm), F32)],
        compiler_params=_cparams("arbitrary", "arbitrary"),
        name="peer_main",
    )(h, u, v, c1, e1, r2, e2, res, gate)


def _rope_tables(seq, head_dim):
    rows = seq // GRID_W
    axis_dim = head_dim // 2
    r = jnp.repeat(jnp.arange(rows, dtype=F32), GRID_W)
    col = jnp.tile(jnp.arange(GRID_W, dtype=F32), rows)
    inv = ROPE_BASE ** (-jnp.arange(0, axis_dim, 2, dtype=F32) / axis_dim)
    ang = jnp.concatenate([r[:, None] * inv, col[:, None] * inv], axis=-1)
    cos, sin = jnp.cos(ang), jnp.sin(ang)
    reps = LANES // head_dim
    cos_l = jnp.tile(jnp.concatenate([cos, cos], axis=-1), (1, reps))
    sin_l = jnp.tile(jnp.concatenate([-sin, sin], axis=-1), (1, reps))
    return cos_l, sin_l


def kernel(x, c, ctx, c_ctx, w_ada, b_ada, norm_mix_g, norm_ffn_g, w_in, rg_conv_w, rg_conv_b, rg_wa, rg_ba, rg_wx, rg_bx, rg_lambda, da_lq, da_lk, da_subln_g, hg_lb, hg_onorm_g, w_branch, b_gate, w_out, peer_wq, peer_subkeys, peer_u, peer_v, final_norm_g):
    bsz, s, d = x.shape
    n_ctx = ctx.shape[1]
    depth = w_in.shape[0]
    head_dim = da_lq.shape[-1]
    da_heads = BRANCH_COLS // (2 * head_dim)
    peer_heads = peer_wq.shape[-1] // (2 * peer_subkeys.shape[-1])
    assert bsz + 1 <= SUBLANES and 2 * head_dim == LANES

    xl = x.reshape(bsz * s, d)
    xc = ctx.reshape(bsz * n_ctx, d)
    cvec = jnp.concatenate([c, c_ctx[None, :], jnp.zeros((SUBLANES - bsz - 1, d), F32)], axis=0)
    mods = ada_mods(cvec, w_ada, b_ada).reshape(depth, SUBLANES, N_MOD, d)
    lb = jnp.cumsum(jax.nn.softmax(hg_lb.astype(F32), axis=1), axis=1)
    lb = lb - lb[:, :1]
    cos_l, sin_l = _rope_tables(s, head_dim)
    scale = head_dim ** -0.5
    u_bf = peer_u.astype(BF16)
    v_bf = peer_v.astype(BF16)

    for l in range(depth):
        last = l == depth - 1
        ml = [mods[l, :bsz, k][:, None, :] for k in range(N_MOD)]
        mc = [mods[l, bsz:bsz + 1, k][:, None, :] for k in range(N_MOD)]
        lam_init = 0.8 - 0.6 * math.exp(-0.3 * l)
        lam = jnp.exp(jnp.sum(da_lq[l, 0] * da_lk[l, 0])) - jnp.exp(jnp.sum(da_lq[l, 1] * da_lk[l, 1])) + lam_init
        par = jnp.concatenate([jnp.full((1, LANES), lam, F32), da_subln_g[l][None, :],
                               jnp.full((1, LANES), 1.0 - lam_init, F32),
                               jnp.zeros((SUBLANES - 3, LANES), F32)], axis=0)

        hl = norm_mod(xl, norm_mix_g[l], ml[0], ml[1], s)
        hc = norm_mod(xc, norm_mix_g[l], mc[0], mc[1], bsz * n_ctx)
        pj_l = matmul(hl, w_in, l, F32)
        pj_c = matmul(hc, w_in, l, F32)
        rg_l, rg_c = rglru(pj_l, pj_c, bsz, rg_conv_w[l], rg_conv_b[l], rg_wa[l], rg_ba[l], rg_wx[l], rg_bx[l],
                           rg_lambda[l])
        ql, kl, vl = qkv_prep(pj_l, cos_l, sin_l, s, scale, True)
        qc, kc, vc = qkv_prep(pj_c, cos_l, sin_l, n_ctx, scale, False)
        da_l = diff_attention(ql, kc, vc, kl, vl, par, bsz, da_heads)
        hg_l, hg_c = hgrn(pj_l, pj_c, bsz, lb[:, l], hg_onorm_g[l])
        mg_l = merge((rg_l, da_l, hg_l), pj_l, w_branch, l, b_gate[l])
        xl = matmul_residual(mg_l, w_out, l, xl, ml[2], s)
        if not last:
            da_c = diff_attention(qc, kc, vc, None, None, par, bsz, da_heads)
            mg_c = merge((rg_c, da_c, hg_c), pj_c, w_branch, l, b_gate[l])
            xc = matmul_residual(mg_c, w_out, l, xc, mc[2], bsz * n_ctx)

        hl = norm_mod(xl, norm_ffn_g[l], ml[3], ml[4], s)
        tabs = peer_tables(matmul(hl, peer_wq, l, F32), peer_subkeys[l], peer_heads)
        xl = peer_main(hl, u_bf, v_bf, l, tabs, xl, ml[5], s)
        if not last:
            hc = norm_mod(xc, norm_ffn_g[l], mc[3], mc[4], bsz * n_ctx)
            tabs = peer_tables(matmul(hc, peer_wq, l, F32), peer_subkeys[l], peer_heads)
            xc = peer_main(hc, u_bf, v_bf, l, tabs, xc, mc[5], bsz * n_ctx)

    return plain_norm(xl, final_norm_g).reshape(bsz, s, d)
```

```python
import functools
import math

import jax
import jax.numpy as jnp
from jax import lax
from jax.experimental import pallas as pl
from jax.experimental.pallas import tpu as pltpu

F32 = jnp.float32
BF16 = jnp.bfloat16

N_MOD = 6
EPS = 1e-6
GRID_W = 64
ROPE_BASE = 10000.0
RG_C = 8.0
LOGF_FLOOR = 1e-20
HG_CHUNK = 64
PEER_TOPK = 16
BRANCH_COLS = 1024
LANES = 128
SUBLANES = 8
VMEM_LIMIT_BYTES = 56 * 1024 * 1024

COL_RG_X, COL_RG_GATE, COL_DA_Q, COL_DA_K, COL_DA_V, COL_HG_Q, COL_HG_FF, COL_HG_FB, COL_HG_I, COL_HG_G, COL_GATES = range(11)


def _cparams(*sem):
    return pltpu.CompilerParams(dimension_semantics=sem, vmem_limit_bytes=VMEM_LIMIT_BYTES)


def _tile(n, pref):
    t = min(n, pref)
    assert n % t == 0, (n, pref)
    return t


def _gelu(x):
    return 0.5 * x * (1.0 + jnp.tanh(math.sqrt(2.0 / math.pi) * (x + 0.044715 * (x * x * x))))


def _sigmoid(x):
    return 1.0 / (1.0 + jnp.exp(-x))


def _dot(a, b):
    return jnp.dot(a, b, preferred_element_type=F32)


def _dot_nt(a, b):
    return lax.dot_general(a, b, (((1,), (1,)), ((), ())), preferred_element_type=F32)


def _dot_tn(a, b):
    return lax.dot_general(a, b, (((0,), (0,)), ((), ())), preferred_element_type=F32)


def _ada_kernel(c_ref, w_ref, b_ref, o_ref):
    c = c_ref[...]
    sc = c * _sigmoid(c)
    o_ref[...] = _dot(sc.astype(BF16), w_ref[...].astype(BF16)) + b_ref[...]


def ada_mods(cvec, w_ada, b_ada):
    depth, d, n = w_ada.shape
    tn = _tile(n, 1024)
    return pl.pallas_call(
        _ada_kernel,
        grid=(depth, n // tn),
        in_specs=[pl.BlockSpec((SUBLANES, d), lambda l, j: (0, 0)),
                  pl.BlockSpec((None, d, tn), lambda l, j: (l, 0, j)),
                  pl.BlockSpec((None, 1, tn), lambda l, j: (l, 0, j))],
        out_specs=pl.BlockSpec((None, SUBLANES, tn), lambda l, j: (l, 0, j)),
        out_shape=jax.ShapeDtypeStruct((depth, SUBLANES, n), F32),
        compiler_params=_cparams("arbitrary", "arbitrary"),
        name="ada_mods",
    )(cvec, w_ada, b_ada.reshape(depth, 1, n))


def _norm_kernel(x_ref, g_ref, sh_ref, sc_ref, o_ref):
    x = x_ref[...]
    y = x * lax.rsqrt(jnp.mean(x * x, axis=-1, keepdims=True) + EPS) * g_ref[...]
    o_ref[...] = (y * (1.0 + sc_ref[...]) + sh_ref[...]).astype(o_ref.dtype)


def _plain_norm_kernel(x_ref, g_ref, o_ref):
    x = x_ref[...]
    o_ref[...] = (x * lax.rsqrt(jnp.mean(x * x, axis=-1, keepdims=True) + EPS) * g_ref[...]).astype(o_ref.dtype)


def norm_mod(x, g, shift, scale, rows_per_src):
    t, d = x.shape
    tr = _tile(rows_per_src, 512)
    src = lambda i: (i * tr // rows_per_src, 0, 0)
    return pl.pallas_call(
        _norm_kernel,
        grid=(t // tr,),
        in_specs=[pl.BlockSpec((tr, d), lambda i: (i, 0)),
                  pl.BlockSpec((1, d), lambda i: (0, 0)),
                  pl.BlockSpec((None, 1, d), src),
                  pl.BlockSpec((None, 1, d), src)],
        out_specs=pl.BlockSpec((tr, d), lambda i: (i, 0)),
        out_shape=jax.ShapeDtypeStruct((t, d), BF16),
        compiler_params=_cparams("arbitrary"),
        name="norm_mod",
    )(x, g.reshape(1, d), shift, scale)


def plain_norm(x, g):
    t, d = x.shape
    tr = _tile(t, 512)
    return pl.pallas_call(
        _plain_norm_kernel,
        grid=(t // tr,),
        in_specs=[pl.BlockSpec((tr, d), lambda i: (i, 0)),
                  pl.BlockSpec((1, d), lambda i: (0, 0))],
        out_specs=pl.BlockSpec((tr, d), lambda i: (i, 0)),
        out_shape=jax.ShapeDtypeStruct((t, d), x.dtype),
        compiler_params=_cparams("arbitrary"),
        name="final_norm",
    )(x, g.reshape(1, d))


def _mm_kernel(a_ref, w_ref, o_ref):
    o_ref[...] = _dot(a_ref[...], w_ref[...].astype(BF16)).astype(o_ref.dtype)


def matmul(a, w, layer, out_dtype, tm_pref=1024, tn_pref=512):
    m, k = a.shape
    n = w.shape[2]
    tm, tn = _tile(m, tm_pref), _tile(n, tn_pref)
    return pl.pallas_call(
        _mm_kernel,
        grid=(m // tm, n // tn),
        in_specs=[pl.BlockSpec((tm, k), lambda i, j: (i, 0)),
                  pl.BlockSpec((None, k, tn), lambda i, j: (layer, 0, j))],
        out_specs=pl.BlockSpec((tm, tn), lambda i, j: (i, j)),
        out_shape=jax.ShapeDtypeStruct((m, n), out_dtype),
        compiler_params=_cparams("arbitrary", "arbitrary"),
        name="matmul",
    )(a, w)


def _mm_res_kernel(a_ref, w_ref, res_ref, gate_ref, o_ref):
    o_ref[...] = res_ref[...] + gate_ref[...] * _dot(a_ref[...], w_ref[...].astype(BF16))


def matmul_residual(a, w, layer, res, gate, rows_per_src):
    m, k = a.shape
    n = w.shape[2]
    tm, tn = _tile(rows_per_src, 1024), _tile(n, 512)
    return pl.pallas_call(
        _mm_res_kernel,
        grid=(m // tm, n // tn),
        in_specs=[pl.BlockSpec((tm, k), lambda i, j: (i, 0)),
                  pl.BlockSpec((None, k, tn), lambda i, j: (layer, 0, j)),
                  pl.BlockSpec((tm, tn), lambda i, j: (i, j)),
                  pl.BlockSpec((None, 1, tn), lambda i, j: (i * tm // rows_per_src, 0, j))],
        out_specs=pl.BlockSpec((tm, tn), lambda i, j: (i, j)),
        out_shape=jax.ShapeDtypeStruct((m, n), F32),
        compiler_params=_cparams("arbitrary", "arbitrary"),
        name="matmul_residual",
    )(a, w, res, gate)


def _rglru_kernel(xl_ref, gl_ref, xc_ref, gc_ref, cw_ref, cb_ref, wa_ref, ba_ref, wx_ref, bx_ref, lam_ref,
                  yl_ref, yc_ref, al_ref, bl_ref, ac_ref, bc_ref, hl_ref, hc_ref):
    cw = cw_ref[...]
    cb = cb_ref[...]

    def coeffs(x_ref, a_ref, b_ref):
        n = x_ref.shape[0]
        x = x_ref[...]
        row = lax.broadcasted_iota(jnp.int32, x.shape, 0)
        xm2 = jnp.where(row >= 2, pltpu.roll(x, 2, 0), 0.0)
        xm1 = jnp.where(row >= 1, pltpu.roll(x, 1, 0), 0.0)
        xp1 = jnp.where(row < n - 1, pltpu.roll(x, n - 1, 0), 0.0)
        u = cb + cw[0:1] * xm2
        u = u + cw[1:2] * xm1
        u = u + cw[2:3] * x
        u = u + cw[3:4] * xp1
        ub = u.astype(BF16)
        for d in range(2):
            r = _sigmoid(_dot(ub, wa_ref[d].astype(BF16)) + ba_ref[d:d + 1, :])
            i = _sigmoid(_dot(ub, wx_ref[d].astype(BF16)) + bx_ref[d:d + 1, :])
            lam = lam_ref[d:d + 1, :]
            softplus_neg_lam = jnp.maximum(-lam, 0.0) + jnp.log(1.0 + jnp.exp(-jnp.abs(lam)))
            log_a = -RG_C * r * softplus_neg_lam
            a_ref[d] = jnp.exp(log_a)
            th = jnp.tanh(log_a)
            b_ref[d] = jnp.sqrt(-2.0 * th / (1.0 - th)) * (i * u)

    row8 = lax.broadcasted_iota(jnp.int32, (SUBLANES, LANES), 0)

    group = 4

    def local_scan(a, b, reverse):
        for s in (1, 2, 4):
            if reverse:
                m = row8 < SUBLANES - s
                b = jnp.where(m, a * pltpu.roll(b, SUBLANES - s, 0) + b, b)
                a = jnp.where(m, a * pltpu.roll(a, SUBLANES - s, 0), a)
            else:
                m = row8 >= s
                b = jnp.where(m, a * pltpu.roll(b, s, 0) + b, b)
                a = jnp.where(m, a * pltpu.roll(a, s, 0), a)
        return a, b

    def scan(a_ref, b_ref, h_ref, carry):
        rows = group * SUBLANES
        n_it = a_ref.shape[1] // rows

        def body(i, carry):
            cf, cbk = carry
            r0 = pl.multiple_of(i * rows, rows)
            r1 = pl.multiple_of((n_it - 1 - i) * rows, rows)
            fwd = [local_scan(a_ref[0, pl.ds(r0 + v * SUBLANES, SUBLANES), :],
                              b_ref[0, pl.ds(r0 + v * SUBLANES, SUBLANES), :], False) for v in range(group)]
            bwd = [local_scan(a_ref[1, pl.ds(r1 + v * SUBLANES, SUBLANES), :],
                              b_ref[1, pl.ds(r1 + v * SUBLANES, SUBLANES), :], True) for v in range(group)]
            for v in range(group):
                a, b = fwd[v]
                h = b + a * cf
                h_ref[0, pl.ds(r0 + v * SUBLANES, SUBLANES), :] = h
                cf = h[SUBLANES - 1:SUBLANES, :]
                a, b = bwd[group - 1 - v]
                h = b + a * cbk
                h_ref[1, pl.ds(r1 + (group - 1 - v) * SUBLANES, SUBLANES), :] = h
                cbk = h[0:1, :]
            return cf, cbk

        return lax.fori_loop(0, n_it, body, carry)

    coeffs(xc_ref, ac_ref, bc_ref)
    coeffs(xl_ref, al_ref, bl_ref)
    zero = jnp.zeros((1, LANES), F32)
    carry = scan(ac_ref, bc_ref, hc_ref, (zero, zero))
    scan(al_ref, bl_ref, hl_ref, carry)
    yc_ref[...] = ((hc_ref[0] + hc_ref[1]) * _gelu(gc_ref[...])).astype(yc_ref.dtype)
    yl_ref[...] = ((hl_ref[0] + hl_ref[1]) * _gelu(gl_ref[...])).astype(yl_ref.dtype)


def rglru(proj_l, proj_c, bsz, conv_w, conv_b, wa, ba, wx, bx, lam):
    s = proj_l.shape[0] // bsz
    n_ctx = proj_c.shape[0] // bsz
    nblk, bd = wa.shape[1], wa.shape[2]
    assert bd == LANES
    width = nblk * bd
    cpb = BRANCH_COLS // bd
    row_l = lambda col: pl.BlockSpec((s, bd), lambda b, j: (b, col * cpb + j))
    row_c = lambda col: pl.BlockSpec((n_ctx, bd), lambda b, j: (b, col * cpb + j))
    vec = lambda r: pl.BlockSpec((r, bd), lambda b, j: (0, j))
    wspec = pl.BlockSpec((2, None, bd, bd), lambda b, j: (0, j, 0, 0))
    return pl.pallas_call(
        _rglru_kernel,
        grid=(bsz, nblk),
        in_specs=[row_l(COL_RG_X), row_l(COL_RG_GATE), row_c(COL_RG_X), row_c(COL_RG_GATE),
                  vec(conv_w.shape[0]), vec(1), wspec, vec(2), wspec, vec(2), vec(2)],
        out_specs=[pl.BlockSpec((s, bd), lambda b, j: (b, j)),
                   pl.BlockSpec((n_ctx, bd), lambda b, j: (b, j))],
        out_shape=[jax.ShapeDtypeStruct((bsz * s, width), BF16),
                   jax.ShapeDtypeStruct((bsz * n_ctx, width), BF16)],
        scratch_shapes=[pltpu.VMEM((2, s, bd), F32), pltpu.VMEM((2, s, bd), F32),
                        pltpu.VMEM((2, n_ctx, bd), F32), pltpu.VMEM((2, n_ctx, bd), F32),
                        pltpu.VMEM((2, s, bd), F32), pltpu.VMEM((2, n_ctx, bd), F32)],
        compiler_params=_cparams("arbitrary", "arbitrary"),
        name="rglru",
    )(proj_l, proj_l, proj_c, proj_c, conv_w, conv_b.reshape(1, width), wa, ba, wx, bx, lam)


def _qkv_prep_kernel(q_ref, k_ref, v_ref, cos_ref, sin_ref, qo_ref, ko_ref, vo_ref, *, scale, rope):
    tr = q_ref.shape[0]
    lane = lax.broadcasted_iota(jnp.int32, (tr, LANES), 1)
    first_half = (lane % 64) < 32
    cos = cos_ref[...]
    sin = sin_ref[...]
    for g in range(q_ref.shape[1] // LANES):
        sl = slice(g * LANES, (g + 1) * LANES)
        for src, dst, mult in ((q_ref, qo_ref, scale), (k_ref, ko_ref, None)):
            x = src[:, sl]
            if rope:
                partner = jnp.where(first_half, pltpu.roll(x, LANES - 32, 1), pltpu.roll(x, 32, 1))
                x = x * cos + partner * sin
            if mult is not None:
                x = x * mult
            dst[:, sl] = x.astype(dst.dtype)
    vo_ref[...] = v_ref[...].astype(vo_ref.dtype)


def qkv_prep(proj, cos, sin, seq, scale, rope):
    t = proj.shape[0]
    tr = _tile(seq, 512)
    col = lambda c: pl.BlockSpec((tr, BRANCH_COLS), lambda i: (i, c))
    tab = pl.BlockSpec((tr, LANES), lambda i: (i % (seq // tr), 0))
    out = jax.ShapeDtypeStruct((t, BRANCH_COLS), BF16)
    ospec = pl.BlockSpec((tr, BRANCH_COLS), lambda i: (i, 0))
    return pl.pallas_call(
        functools.partial(_qkv_prep_kernel, scale=scale, rope=rope),
        grid=(t // tr,),
        in_specs=[col(COL_DA_Q), col(COL_DA_K), col(COL_DA_V), tab, tab],
        out_specs=[ospec, ospec, ospec],
        out_shape=[out, out, out],
        compiler_params=_cparams("arbitrary"),
        name="qkv_prep",
    )(proj, proj, proj, cos, sin)


def _attn_kernel(*refs, has_latent):
    if has_latent:
        q_ref, kc_ref, vc_ref, kl_ref, vl_ref, par_ref, o_ref = refs
    else:
        q_ref, kc_ref, vc_ref, par_ref, o_ref = refs
    q = q_ref[...]
    lane = lax.broadcasted_iota(jnp.int32, q.shape, 1)
    outs = []
    for j in range(2):
        qj = jnp.where((lane // 64) == j, q, jnp.zeros_like(q))
        sc = _dot_nt(qj, kc_ref[...])
        m = jnp.max(sc, axis=-1, keepdims=True)
        if has_latent:
            sl = _dot_nt(qj, kl_ref[...])
            m = jnp.maximum(m, jnp.max(sl, axis=-1, keepdims=True))
        pc = jnp.exp2(sc - m)
        den = jnp.sum(pc, axis=-1, keepdims=True)
        acc = _dot(pc.astype(BF16), vc_ref[...])
        if has_latent:
            pl_ = jnp.exp2(sl - m)
            den = den + jnp.sum(pl_, axis=-1, keepdims=True)
            acc = acc + _dot(pl_.astype(BF16), vl_ref[...])
        outs.append(acc / den)
    o = outs[0] - par_ref[0:1, :] * outs[1]
    y = o * lax.rsqrt(jnp.mean(o * o, axis=-1, keepdims=True) + EPS) * par_ref[1:2, :]
    o_ref[...] = (y * par_ref[2:3, :]).astype(o_ref.dtype)


def diff_attention(q, kc, vc, kl, vl, par, bsz, n_heads):
    t, width = q.shape
    hw = width // n_heads
    seq = t // bsz
    n_ctx = kc.shape[0] // bsz
    tq = _tile(seq, 256)
    nq = seq // tq
    has_latent = kl is not None
    in_specs = [pl.BlockSpec((tq, hw), lambda b, h, i: (b * nq + i, h)),
                pl.BlockSpec((n_ctx, hw), lambda b, h, i: (b, h)),
                pl.BlockSpec((n_ctx, hw), lambda b, h, i: (b, h))]
    args = [q, kc, vc]
    if has_latent:
        s = kl.shape[0] // bsz
        in_specs += [pl.BlockSpec((s, hw), lambda b, h, i: (b, h))] * 2
        args += [kl, vl]
    in_specs.append(pl.BlockSpec((SUBLANES, hw), lambda b, h, i: (0, 0)))
    args.append(par)
    return pl.pallas_call(
        functools.partial(_attn_kernel, has_latent=has_latent),
        grid=(bsz, n_heads, nq),
        in_specs=in_specs,
        out_specs=pl.BlockSpec((tq, hw), lambda b, h, i: (b * nq + i, h)),
        out_shape=jax.ShapeDtypeStruct((t, width), BF16),
        compiler_params=_cparams("arbitrary", "arbitrary", "arbitrary"),
        name="diff_attention",
    )(*args)


def _hgrn_kernel(ql_ref, zfl_ref, zbl_ref, vl_ref, gl_ref, qc_ref, zfc_ref, zbc_ref, vc_ref, gc_ref,
                 lb_ref, on_ref, yl_ref, yc_ref, ol_ref, oc_ref, st_ref):
    c = HG_CHUNK
    row = lax.broadcasted_iota(jnp.int32, (c, LANES), 0)
    ti = lax.broadcasted_iota(jnp.int32, (c, c), 0)
    si = lax.broadcasted_iota(jnp.int32, (c, c), 1)

    def chunk(q, z, v, d, reverse):
        lb = lb_ref[d:d + 1, :]
        f = lb + (1.0 - lb) * _sigmoid(z)
        kk = (1.0 - lb) * _sigmoid(-z)
        cum = jnp.log(jnp.maximum(f, LOGF_FLOOR))
        for s in (1, 2, 4, 8, 16, 32):
            if reverse:
                cum = cum + jnp.where(row < c - s, pltpu.roll(cum, c - s, 0), 0.0)
            else:
                cum = cum + jnp.where(row >= s, pltpu.roll(cum, s, 0), 0.0)
        mid = cum[c // 2:c // 2 + 1, :]
        c_end = cum[0:1, :] if reverse else cum[c - 1:c, :]
        qt = (q * jnp.exp(cum - mid)).astype(BF16)
        kt = (kk * jnp.exp(mid - cum)).astype(BF16)
        att = _dot_nt(qt, kt)
        att = jnp.where((ti <= si) if reverse else (ti >= si), att, 0.0)
        vb = v.astype(BF16)
        state = st_ref[d]
        o = _dot(att.astype(BF16), vb) + _dot_nt((q * jnp.exp(cum)).astype(BF16), state.astype(BF16))
        kd = (kk * jnp.exp(c_end - cum)).astype(BF16)
        st_ref[d] = state * jnp.exp(c_end) + _dot_tn(vb, kd)
        return o

    def segment(q_ref, zf_ref, zb_ref, v_ref, o_ref):
        n = q_ref.shape[0] // c

        def body(i, _):
            r0 = pl.multiple_of(i * c, c)
            o_ref[0, pl.ds(r0, c), :] = chunk(q_ref[pl.ds(r0, c), :], zf_ref[pl.ds(r0, c), :],
                                              v_ref[pl.ds(r0, c), :], 0, False)
            r1 = pl.multiple_of((n - 1 - i) * c, c)
            o_ref[1, pl.ds(r1, c), :] = chunk(q_ref[pl.ds(r1, c), :], zb_ref[pl.ds(r1, c), :],
                                              v_ref[pl.ds(r1, c), :], 1, True)
            return 0

        lax.fori_loop(0, n, body, 0, unroll=2)

    def finish(o_ref, g_ref, y_ref):
        o = o_ref[0] + o_ref[1]
        y = o * lax.rsqrt(jnp.mean(o * o, axis=-1, keepdims=True) + EPS) * on_ref[...]
        g = g_ref[...]
        y_ref[...] = (y * (g * _sigmoid(g))).astype(y_ref.dtype)

    st_ref[...] = jnp.zeros_like(st_ref)
    segment(qc_ref, zfc_ref, zbc_ref, vc_ref, oc_ref)
    segment(ql_ref, zfl_ref, zbl_ref, vl_ref, ol_ref)
    finish(oc_ref, gc_ref, yc_ref)
    finish(ol_ref, gl_ref, yl_ref)


def hgrn(proj_l, proj_c, bsz, lb, onorm_g):
    s = proj_l.shape[0] // bsz
    n_ctx = proj_c.shape[0] // bsz
    dv = onorm_g.shape[0]
    assert dv == LANES and s % HG_CHUNK == 0 and n_ctx % HG_CHUNK == 0
    n_heads = BRANCH_COLS // dv
    row_l = lambda col: pl.BlockSpec((s, dv), lambda b, h: (b, col * n_heads + h))
    row_c = lambda col: pl.BlockSpec((n_ctx, dv), lambda b, h: (b, col * n_heads + h))
    cols = (COL_HG_Q, COL_HG_FF, COL_HG_FB, COL_HG_I, COL_HG_G)
    return pl.pallas_call(
        _hgrn_kernel,
        grid=(bsz, n_heads),
        in_specs=[row_l(cc) for cc in cols] + [row_c(cc) for cc in cols]
        + [pl.BlockSpec((2, dv), lambda b, h: (0, h)), pl.BlockSpec((1, dv), lambda b, h: (0, 0))],
        out_specs=[pl.BlockSpec((s, dv), lambda b, h: (b, h)),
                   pl.BlockSpec((n_ctx, dv), lambda b, h: (b, h))],
        out_shape=[jax.ShapeDtypeStruct((bsz * s, BRANCH_COLS), BF16),
                   jax.ShapeDtypeStruct((bsz * n_ctx, BRANCH_COLS), BF16)],
        scratch_shapes=[pltpu.VMEM((2, s, dv), F32), pltpu.VMEM((2, n_ctx, dv), F32),
                        pltpu.VMEM((2, dv, dv), F32)],
        compiler_params=_cparams("arbitrary", "arbitrary"),
        name="hgrn",
    )(*([proj_l] * 5 + [proj_c] * 5 + [lb, onorm_g.reshape(1, dv)]))


def _merge_kernel(y0_ref, y1_ref, y2_ref, g0_ref, g1_ref, g2_ref, w_ref, bg_ref, o_ref):
    acc = None
    for kb, (y_ref, g_ref) in enumerate(((y0_ref, g0_ref), (y1_ref, g1_ref), (y2_ref, g2_ref))):
        gate = _sigmoid(g_ref[...] + bg_ref[kb:kb + 1, :])
        term = gate * _dot(y_ref[...], w_ref[kb].astype(BF16))
        acc = term if acc is None else acc + term
    o_ref[...] = acc.astype(o_ref.dtype)


def merge(ys, proj, w_branch, layer, b_gate):
    t, k = ys[0].shape
    _, nb, _, d = w_branch.shape
    tm, tn = _tile(t, 1024), _tile(d, 256)
    gate_col0 = COL_GATES * BRANCH_COLS // tn
    yspec = pl.BlockSpec((tm, k), lambda i, j: (i, 0))
    gspec = lambda kb: pl.BlockSpec((tm, tn), lambda i, j: (i, gate_col0 + kb * (d // tn) + j))
    return pl.pallas_call(
        _merge_kernel,
        grid=(t // tm, d // tn),
        in_specs=[yspec, yspec, yspec, gspec(0), gspec(1), gspec(2),
                  pl.BlockSpec((None, nb, k, tn), lambda i, j: (layer, 0, 0, j)),
                  pl.BlockSpec((nb, tn), lambda i, j: (0, j))],
        out_specs=pl.BlockSpec((tm, tn), lambda i, j: (i, j)),
        out_shape=jax.ShapeDtypeStruct((t, d), BF16),
        compiler_params=_cparams("arbitrary", "arbitrary"),
        name="merge",
    )(ys[0], ys[1], ys[2], proj, proj, proj, w_branch, b_gate)


def _peer_tables_kernel(q_ref, sub_ref, c1_ref, e1_ref, r2_ref, e2_ref, top_ref, s_ref, work_ref):
    _, n_heads, n_keys, tt = s_ref.shape
    neg_inf = jnp.float32(-jnp.inf)
    for j in range(2):
        sub = sub_ref[j].astype(BF16)
        for h in range(n_heads):
            c0 = (h * 2 + j) * n_keys
            s = _dot_nt(sub, q_ref[:, c0:c0 + n_keys].astype(BF16))
            s_ref[j, h] = s
            work_ref[h] = s

        def extract(i, _, j=j):
            for h in range(n_heads):
                work = work_ref[h]
                m = jnp.max(work, axis=0, keepdims=True)
                top_ref[j, i, pl.ds(h, 1), :] = m
                work_ref[h] = jnp.where(work == m, neg_inf, work)
            return 0

        lax.fori_loop(0, PEER_TOPK, extract, 0)
    a = [top_ref[0, i] for i in range(PEER_TOPK)]
    b = [top_ref[1, i] for i in range(PEER_TOPK)]
    cands = [a[i] + b[jj] for i in range(PEER_TOPK) for jj in range(PEER_TOPK) if (i + 1) * (jj + 1) <= PEER_TOPK]
    tops = []
    for _ in range(PEER_TOPK):
        m = functools.reduce(jnp.maximum, cands)
        tops.append(m)
        taken = None
        nxt = []
        for cand in cands:
            eq = cand == m
            hit = eq if taken is None else jnp.logical_and(eq, jnp.logical_not(taken))
            nxt.append(jnp.where(hit, neg_inf, cand))
            taken = eq if taken is None else jnp.logical_or(taken, eq)
        cands = nxt
    tau = tops[-1]
    z = functools.reduce(lambda p, q: p + q, [jnp.exp(t - tops[0]) for t in tops])
    inv_z = 1.0 / z
    count = []
    for i in range(PEER_TOPK):
        n = None
        for jj in range(PEER_TOPK // (i + 1)):
            hit = jnp.where(a[i] + b[jj] >= tau, 1.0, 0.0)
            n = hit if n is None else n + hit
        count.append(n)
    for h in range(n_heads):
        s1 = s_ref[0, h]
        s2 = s_ref[1, h]
        c1 = jnp.zeros_like(s1)
        rank2 = jnp.zeros_like(s2)
        for i in range(PEER_TOPK):
            c1 = jnp.where(s1 == a[i][h:h + 1, :], count[i][h:h + 1, :], c1)
            rank2 = rank2 + jnp.where(b[i][h:h + 1, :] > s2, 1.0, 0.0)
        c1_ref[h] = c1
        r2_ref[h] = rank2.astype(r2_ref.dtype)
        e1_ref[h] = jnp.exp(s1 - a[0][h:h + 1, :]) * inv_z[h:h + 1, :]
        e2_ref[h] = jnp.exp(s2 - b[0][h:h + 1, :]).astype(e2_ref.dtype)


def peer_tables(q, subkeys, n_heads):
    t = q.shape[0]
    n_keys = subkeys.shape[1]
    assert n_keys == LANES and n_heads == SUBLANES
    tt = _tile(t, 256)
    tspec = pl.BlockSpec((n_heads, n_keys, tt), lambda i: (0, 0, i))
    tab = lambda dt: jax.ShapeDtypeStruct((n_heads, n_keys, t), dt)
    return pl.pallas_call(
        _peer_tables_kernel,
        grid=(t // tt,),
        in_specs=[pl.BlockSpec((tt, q.shape[1]), lambda i: (i, 0)),
                  pl.BlockSpec(subkeys.shape, lambda i: (0, 0, 0))],
        out_specs=[tspec, tspec, tspec, tspec],
        out_shape=[tab(F32), tab(F32), tab(BF16), tab(BF16)],
        scratch_shapes=[pltpu.VMEM((2, PEER_TOPK, n_heads, tt), F32),
                        pltpu.VMEM((2, n_heads, n_keys, tt), F32),
                        pltpu.VMEM((n_heads, n_keys, tt), F32)],
        compiler_params=_cparams("arbitrary"),
        name="peer_tables",
    )(q, subkeys)


def _peer_main_kernel(h_ref, u_ref, v_ref, c1_ref, e1_ref, r2_ref, e2_ref, res_ref, gate_ref, o_ref, w_ref):
    e = pl.program_id(1)
    groups, n_heads, tm = c1_ref.shape
    n_keys = r2_ref.shape[1]

    @pl.when(e == 0)
    def _():
        o_ref[...] = jnp.zeros_like(o_ref)

    act = _gelu(_dot_nt(u_ref[...], h_ref[...]))
    for c in range(tm // LANES):
        cs = slice(c * LANES, (c + 1) * LANES)
        for a0 in range(0, groups, 2):
            accs = [None, None]
            for h in range(n_heads):
                r2c = r2_ref[h, :, cs]
                e2c = e2_ref[h, :, cs]
                for k in range(2):
                    c1row = c1_ref[a0 + k, h:h + 1, cs].astype(BF16)
                    e1row = e1_ref[a0 + k, h:h + 1, cs].astype(BF16)
                    term = jnp.where(r2c < c1row, e2c, jnp.zeros_like(e2c)) * e1row
                    accs[k] = term if accs[k] is None else accs[k] + term
            for k in range(2):
                w_ref[(a0 + k) * n_keys:(a0 + k + 1) * n_keys, cs] = accs[k].astype(F32)
    o_ref[...] += _dot((w_ref[...] * act).T.astype(BF16), v_ref[...])

    @pl.when(e == pl.num_programs(1) - 1)
    def _():
        o_ref[...] = res_ref[...] + gate_ref[...] * o_ref[...]


def peer_main(h, u, v, layer, tables, res, gate, rows_per_src):
    c1, e1, r2, e2 = tables
    c1, e1 = jnp.swapaxes(c1, 0, 1), jnp.swapaxes(e1, 0, 1)
    t, d = h.shape
    n_exp = u.shape[1]
    n_heads, n_keys, _ = r2.shape
    tm = _tile(rows_per_src, 512)
    groups = 8
    te = groups * n_keys
    tok = pl.BlockSpec((tm, d), lambda i, e: (i, 0))
    exp = pl.BlockSpec((None, te, d), lambda i, e: (layer, e, 0))
    grp = pl.BlockSpec((groups, n_heads, tm), lambda i, e: (e, 0, i))
    full = pl.BlockSpec((n_heads, n_keys, tm), lambda i, e: (0, 0, i))
    return pl.pallas_call(
        _peer_main_kernel,
        grid=(t // tm, n_exp // te),
        in_specs=[tok, exp, exp, grp, grp, full, full, tok,
                  pl.BlockSpec((None, 1, d), lambda i, e: (i * tm // rows_per_src, 0, 0))],
        out_specs=tok,
        out_shape=jax.ShapeDtypeStruct((t, d), F32),
        scratch_shapes=[pltpu.VMEM((te, tm), F32)],
        compiler_params=_cparams("arbitrary", "arbitrary"),
        name="peer_main",
    )(h, u, v, c1, e1, r2, e2, res, gate)


def _rope_tables(seq, head_dim):
    rows = seq // GRID_W
    axis_dim = head_dim // 2
    r = jnp.repeat(jnp.arange(rows, dtype=F32), GRID_W)
    col = jnp.tile(jnp.arange(GRID_W, dtype=F32), rows)
    inv = ROPE_BASE ** (-jnp.arange(0, axis_dim, 2, dtype=F32) / axis_dim)
    ang = jnp.concatenate([r[:, None] * inv, col[:, None] * inv], axis=-1)
    cos, sin = jnp.cos(ang), jnp.sin(ang)
    reps = LANES // head_dim
    cos_l = jnp.tile(jnp.concatenate([cos, cos], axis=-1), (1, reps))
    sin_l = jnp.tile(jnp.concatenate([-sin, sin], axis=-1), (1, reps))
    return cos_l, sin_l


def kernel(x, c, ctx, c_ctx, w_ada, b_ada, norm_mix_g, norm_ffn_g, w_in, rg_conv_w, rg_conv_b, rg_wa, rg_ba, rg_wx, rg_bx, rg_lambda, da_lq, da_lk, da_subln_g, hg_lb, hg_onorm_g, w_branch, b_gate, w_out, peer_wq, peer_subkeys, peer_u, peer_v, final_norm_g):
    bsz, s, d = x.shape
    n_ctx = ctx.shape[1]
    depth = w_in.shape[0]
    head_dim = da_lq.shape[-1]
    da_heads = BRANCH_COLS // (2 * head_dim)
    peer_heads = peer_wq.shape[-1] // (2 * peer_subkeys.shape[-1])
    assert bsz + 1 <= SUBLANES and 2 * head_dim == LANES

    xl = x.reshape(bsz * s, d)
    xc = ctx.reshape(bsz * n_ctx, d)
    cvec = jnp.concatenate([c, c_ctx[None, :], jnp.zeros((SUBLANES - bsz - 1, d), F32)], axis=0)
    mods = ada_mods(cvec, w_ada, b_ada).reshape(depth, SUBLANES, N_MOD, d)
    lb = jnp.cumsum(jax.nn.softmax(hg_lb.astype(F32), axis=1), axis=1)
    lb = lb - lb[:, :1]
    cos_l, sin_l = _rope_tables(s, head_dim)
    scale = head_dim ** -0.5 * math.log2(math.e)
    u_bf = peer_u.astype(BF16)
    v_bf = peer_v.astype(BF16)

    for l in range(depth):
        last = l == depth - 1
        ml = [mods[l, :bsz, k][:, None, :] for k in range(N_MOD)]
        mc = [mods[l, bsz:bsz + 1, k][:, None, :] for k in range(N_MOD)]
        lam_init = 0.8 - 0.6 * math.exp(-0.3 * l)
        lam = jnp.exp(jnp.sum(da_lq[l, 0] * da_lk[l, 0])) - jnp.exp(jnp.sum(da_lq[l, 1] * da_lk[l, 1])) + lam_init
        par = jnp.concatenate([jnp.full((1, LANES), lam, F32), da_subln_g[l][None, :],
                               jnp.full((1, LANES), 1.0 - lam_init, F32),
                               jnp.zeros((SUBLANES - 3, LANES), F32)], axis=0)

        hl = norm_mod(xl, norm_mix_g[l], ml[0], ml[1], s)
        hc = norm_mod(xc, norm_mix_g[l], mc[0], mc[1], bsz * n_ctx)
        pj_l = matmul(hl, w_in, l, F32)
        pj_c = matmul(hc, w_in, l, F32)
        rg_l, rg_c = rglru(pj_l, pj_c, bsz, rg_conv_w[l], rg_conv_b[l], rg_wa[l], rg_ba[l], rg_wx[l], rg_bx[l],
                           rg_lambda[l])
        ql, kl, vl = qkv_prep(pj_l, cos_l, sin_l, s, scale, True)
        qc, kc, vc = qkv_prep(pj_c, cos_l, sin_l, n_ctx, scale, False)
        da_l = diff_attention(ql, kc, vc, kl, vl, par, bsz, da_heads)
        hg_l, hg_c = hgrn(pj_l, pj_c, bsz, lb[:, l], hg_onorm_g[l])
        mg_l = merge((rg_l, da_l, hg_l), pj_l, w_branch, l, b_gate[l])
        xl = matmul_residual(mg_l, w_out, l, xl, ml[2], s)
        if not last:
            da_c = diff_attention(qc, kc, vc, None, None, par, bsz, da_heads)
            mg_c = merge((rg_c, da_c, hg_c), pj_c, w_branch, l, b_gate[l])
            xc = matmul_residual(mg_c, w_out, l, xc, mc[2], bsz * n_ctx)

        hl = norm_mod(xl, norm_ffn_g[l], ml[3], ml[4], s)
        tabs = peer_tables(matmul(hl, peer_wq, l, F32), peer_subkeys[l], peer_heads)
        xl = peer_main(hl, u_bf, v_bf, l, tabs, xl, ml[5], s)
        if not last:
            hc = norm_mod(xc, norm_ffn_g[l], mc[3], mc[4], bsz * n_ctx)
            tabs = peer_tables(matmul(hc, peer_wq, l, F32), peer_subkeys[l], peer_heads)
            xc = peer_main(hc, u_bf, v_bf, l, tabs, xc, mc[5], bsz * n_ctx)

    return plain_norm(xl, final_norm_g).reshape(bsz, s, d)
```

```python
import functools
import math

import jax
import jax.numpy as jnp
from jax import lax
from jax.experimental import pallas as pl
from jax.experimental.pallas import tpu as pltpu

F32 = jnp.float32
BF16 = jnp.bfloat16

N_MOD = 6
EPS = 1e-6
GRID_W = 64
ROPE_BASE = 10000.0
RG_C = 8.0
LOGF_FLOOR = 1e-20
HG_CHUNK = 64
PEER_TOPK = 16
BRANCH_COLS = 1024
LANES = 128
SUBLANES = 8
VMEM_LIMIT_BYTES = 56 * 1024 * 1024

COL_RG_X, COL_RG_GATE, COL_DA_Q, COL_DA_K, COL_DA_V, COL_HG_Q, COL_HG_FF, COL_HG_FB, COL_HG_I, COL_HG_G, COL_GATES = range(11)


def _cparams(*sem):
    return pltpu.CompilerParams(dimension_semantics=sem, vmem_limit_bytes=VMEM_LIMIT_BYTES)


def _tile(n, pref):
    t = min(n, pref)
    assert n % t == 0, (n, pref)
    return t


def _gelu(x):
    return 0.5 * x * (1.0 + jnp.tanh(math.sqrt(2.0 / math.pi) * (x + 0.044715 * (x * x * x))))


def _sigmoid(x):
    return 1.0 / (1.0 + jnp.exp(-x))


def _dot(a, b):
    return jnp.dot(a, b, preferred_element_type=F32)


def _dot_nt(a, b):
    return lax.dot_general(a, b, (((1,), (1,)), ((), ())), preferred_element_type=F32)


def _dot_tn(a, b):
    return lax.dot_general(a, b, (((0,), (0,)), ((), ())), preferred_element_type=F32)


def _ada_kernel(c_ref, w_ref, b_ref, o_ref):
    c = c_ref[...]
    sc = c * _sigmoid(c)
    o_ref[...] = _dot(sc.astype(BF16), w_ref[...].astype(BF16)) + b_ref[...]


def ada_mods(cvec, w_ada, b_ada):
    depth, d, n = w_ada.shape
    tn = _tile(n, 1024)
    return pl.pallas_call(
        _ada_kernel,
        grid=(depth, n // tn),
        in_specs=[pl.BlockSpec((SUBLANES, d), lambda l, j: (0, 0)),
                  pl.BlockSpec((None, d, tn), lambda l, j: (l, 0, j)),
                  pl.BlockSpec((None, 1, tn), lambda l, j: (l, 0, j))],
        out_specs=pl.BlockSpec((None, SUBLANES, tn), lambda l, j: (l, 0, j)),
        out_shape=jax.ShapeDtypeStruct((depth, SUBLANES, n), F32),
        compiler_params=_cparams("arbitrary", "arbitrary"),
        name="ada_mods",
    )(cvec, w_ada, b_ada.reshape(depth, 1, n))


def _norm_kernel(x_ref, g_ref, sh_ref, sc_ref, o_ref):
    x = x_ref[...]
    y = x * lax.rsqrt(jnp.mean(x * x, axis=-1, keepdims=True) + EPS) * g_ref[...]
    o_ref[...] = (y * (1.0 + sc_ref[...]) + sh_ref[...]).astype(o_ref.dtype)


def _plain_norm_kernel(x_ref, g_ref, o_ref):
    x = x_ref[...]
    o_ref[...] = (x * lax.rsqrt(jnp.mean(x * x, axis=-1, keepdims=True) + EPS) * g_ref[...]).astype(o_ref.dtype)


def norm_mod(x, g, shift, scale, rows_per_src):
    t, d = x.shape
    tr = _tile(rows_per_src, 512)
    src = lambda i: (i * tr // rows_per_src, 0, 0)
    return pl.pallas_call(
        _norm_kernel,
        grid=(t // tr,),
        in_specs=[pl.BlockSpec((tr, d), lambda i: (i, 0)),
                  pl.BlockSpec((1, d), lambda i: (0, 0)),
                  pl.BlockSpec((None, 1, d), src),
                  pl.BlockSpec((None, 1, d), src)],
        out_specs=pl.BlockSpec((tr, d), lambda i: (i, 0)),
        out_shape=jax.ShapeDtypeStruct((t, d), BF16),
        compiler_params=_cparams("arbitrary"),
        name="norm_mod",
    )(x, g.reshape(1, d), shift, scale)


def plain_norm(x, g):
    t, d = x.shape
    tr = _tile(t, 512)
    return pl.pallas_call(
        _plain_norm_kernel,
        grid=(t // tr,),
        in_specs=[pl.BlockSpec((tr, d), lambda i: (i, 0)),
                  pl.BlockSpec((1, d), lambda i: (0, 0))],
        out_specs=pl.BlockSpec((tr, d), lambda i: (i, 0)),
        out_shape=jax.ShapeDtypeStruct((t, d), x.dtype),
        compiler_params=_cparams("arbitrary"),
        name="final_norm",
    )(x, g.reshape(1, d))


def _mm_kernel(a_ref, w_ref, o_ref):
    o_ref[...] = _dot(a_ref[...], w_ref[...].astype(BF16)).astype(o_ref.dtype)


def matmul(a, w, layer, out_dtype, tm_pref=1024, tn_pref=512):
    m, k = a.shape
    n = w.shape[2]
    tm, tn = _tile(m, tm_pref), _tile(n, tn_pref)
    return pl.pallas_call(
        _mm_kernel,
        grid=(m // tm, n // tn),
        in_specs=[pl.BlockSpec((tm, k), lambda i, j: (i, 0)),
                  pl.BlockSpec((None, k, tn), lambda i, j: (layer, 0, j))],
        out_specs=pl.BlockSpec((tm, tn), lambda i, j: (i, j)),
        out_shape=jax.ShapeDtypeStruct((m, n), out_dtype),
        compiler_params=_cparams("arbitrary", "arbitrary"),
        name="matmul",
    )(a, w)


def _mm_res_kernel(a_ref, w_ref, res_ref, gate_ref, o_ref):
    o_ref[...] = res_ref[...] + gate_ref[...] * _dot(a_ref[...], w_ref[...].astype(BF16))


def matmul_residual(a, w, layer, res, gate, rows_per_src):
    m, k = a.shape
    n = w.shape[2]
    tm, tn = _tile(rows_per_src, 1024), _tile(n, 512)
    return pl.pallas_call(
        _mm_res_kernel,
        grid=(m // tm, n // tn),
        in_specs=[pl.BlockSpec((tm, k), lambda i, j: (i, 0)),
                  pl.BlockSpec((None, k, tn), lambda i, j: (layer, 0, j)),
                  pl.BlockSpec((tm, tn), lambda i, j: (i, j)),
                  pl.BlockSpec((None, 1, tn), lambda i, j: (i * tm // rows_per_src, 0, j))],
        out_specs=pl.BlockSpec((tm, tn), lambda i, j: (i, j)),
        out_shape=jax.ShapeDtypeStruct((m, n), F32),
        compiler_params=_cparams("arbitrary", "arbitrary"),
        name="matmul_residual",
    )(a, w, res, gate)


def _rglru_kernel(xl_ref, gl_ref, xc_ref, gc_ref, cw_ref, cb_ref, wa_ref, ba_ref, wx_ref, bx_ref, lam_ref,
                  yl_ref, yc_ref, al_ref, bl_ref, ac_ref, bc_ref, hl_ref, hc_ref):
    cw = cw_ref[...]
    cb = cb_ref[...]

    def coeffs(x_ref, a_ref, b_ref):
        n = x_ref.shape[0]
        x = x_ref[...]
        row = lax.broadcasted_iota(jnp.int32, x.shape, 0)
        xm2 = jnp.where(row >= 2, pltpu.roll(x, 2, 0), 0.0)
        xm1 = jnp.where(row >= 1, pltpu.roll(x, 1, 0), 0.0)
        xp1 = jnp.where(row < n - 1, pltpu.roll(x, n - 1, 0), 0.0)
        u = cb + cw[0:1] * xm2
        u = u + cw[1:2] * xm1
        u = u + cw[2:3] * x
        u = u + cw[3:4] * xp1
        ub = u.astype(BF16)
        for d in range(2):
            r = _sigmoid(_dot(ub, wa_ref[d].astype(BF16)) + ba_ref[d:d + 1, :])
            i = _sigmoid(_dot(ub, wx_ref[d].astype(BF16)) + bx_ref[d:d + 1, :])
            lam = lam_ref[d:d + 1, :]
            softplus_neg_lam = jnp.maximum(-lam, 0.0) + jnp.log(1.0 + jnp.exp(-jnp.abs(lam)))
            log_a = -RG_C * r * softplus_neg_lam
            a_ref[d] = jnp.exp(log_a)
            th = jnp.tanh(log_a)
            b_ref[d] = jnp.sqrt(-2.0 * th / (1.0 - th)) * (i * u)

    row8 = lax.broadcasted_iota(jnp.int32, (SUBLANES, LANES), 0)

    group = 4

    def local_scan(a, b, reverse):
        for s in (1, 2, 4):
            if reverse:
                m = row8 < SUBLANES - s
                b = jnp.where(m, a * pltpu.roll(b, SUBLANES - s, 0) + b, b)
                a = jnp.where(m, a * pltpu.roll(a, SUBLANES - s, 0), a)
            else:
                m = row8 >= s
                b = jnp.where(m, a * pltpu.roll(b, s, 0) + b, b)
                a = jnp.where(m, a * pltpu.roll(a, s, 0), a)
        return a, b

    def scan(a_ref, b_ref, h_ref, carry):
        rows = group * SUBLANES
        n_it = a_ref.shape[1] // rows

        def body(i, carry):
            cf, cbk = carry
            r0 = pl.multiple_of(i * rows, rows)
            r1 = pl.multiple_of((n_it - 1 - i) * rows, rows)
            fwd = [local_scan(a_ref[0, pl.ds(r0 + v * SUBLANES, SUBLANES), :],
                              b_ref[0, pl.ds(r0 + v * SUBLANES, SUBLANES), :], False) for v in range(group)]
            bwd = [local_scan(a_ref[1, pl.ds(r1 + v * SUBLANES, SUBLANES), :],
                              b_ref[1, pl.ds(r1 + v * SUBLANES, SUBLANES), :], True) for v in range(group)]
            for v in range(group):
                a, b = fwd[v]
                h = b + a * cf
                h_ref[0, pl.ds(r0 + v * SUBLANES, SUBLANES), :] = h
                cf = h[SUBLANES - 1:SUBLANES, :]
                a, b = bwd[group - 1 - v]
                h = b + a * cbk
                h_ref[1, pl.ds(r1 + (group - 1 - v) * SUBLANES, SUBLANES), :] = h
                cbk = h[0:1, :]
            return cf, cbk

        return lax.fori_loop(0, n_it, body, carry)

    coeffs(xc_ref, ac_ref, bc_ref)
    coeffs(xl_ref, al_ref, bl_ref)
    zero = jnp.zeros((1, LANES), F32)
    carry = scan(ac_ref, bc_ref, hc_ref, (zero, zero))
    scan(al_ref, bl_ref, hl_ref, carry)
    yc_ref[...] = ((hc_ref[0] + hc_ref[1]) * _gelu(gc_ref[...])).astype(yc_ref.dtype)
    yl_ref[...] = ((hl_ref[0] + hl_ref[1]) * _gelu(gl_ref[...])).astype(yl_ref.dtype)


def rglru(proj_l, proj_c, bsz, conv_w, conv_b, wa, ba, wx, bx, lam):
    s = proj_l.shape[0] // bsz
    n_ctx = proj_c.shape[0] // bsz
    nblk, bd = wa.shape[1], wa.shape[2]
    assert bd == LANES
    width = nblk * bd
    cpb = BRANCH_COLS // bd
    row_l = lambda col: pl.BlockSpec((s, bd), lambda b, j: (b, col * cpb + j))
    row_c = lambda col: pl.BlockSpec((n_ctx, bd), lambda b, j: (b, col * cpb + j))
    vec = lambda r: pl.BlockSpec((r, bd), lambda b, j: (0, j))
    wspec = pl.BlockSpec((2, None, bd, bd), lambda b, j: (0, j, 0, 0))
    return pl.pallas_call(
        _rglru_kernel,
        grid=(bsz, nblk),
        in_specs=[row_l(COL_RG_X), row_l(COL_RG_GATE), row_c(COL_RG_X), row_c(COL_RG_GATE),
                  vec(conv_w.shape[0]), vec(1), wspec, vec(2), wspec, vec(2), vec(2)],
        out_specs=[pl.BlockSpec((s, bd), lambda b, j: (b, j)),
                   pl.BlockSpec((n_ctx, bd), lambda b, j: (b, j))],
        out_shape=[jax.ShapeDtypeStruct((bsz * s, width), BF16),
                   jax.ShapeDtypeStruct((bsz * n_ctx, width), BF16)],
        scratch_shapes=[pltpu.VMEM((2, s, bd), F32), pltpu.VMEM((2, s, bd), F32),
                        pltpu.VMEM((2, n_ctx, bd), F32), pltpu.VMEM((2, n_ctx, bd), F32),
                        pltpu.VMEM((2, s, bd), F32), pltpu.VMEM((2, n_ctx, bd), F32)],
        compiler_params=_cparams("arbitrary", "arbitrary"),
        name="rglru",
    )(proj_l, proj_l, proj_c, proj_c, conv_w, conv_b.reshape(1, width), wa, ba, wx, bx, lam)


def _qkv_prep_kernel(q_ref, k_ref, v_ref, cos_ref, sin_ref, qo_ref, ko_ref, vo_ref, *, scale, rope):
    tr = q_ref.shape[0]
    lane = lax.broadcasted_iota(jnp.int32, (tr, LANES), 1)
    first_half = (lane % 64) < 32
    cos = cos_ref[...]
    sin = sin_ref[...]
    for g in range(q_ref.shape[1] // LANES):
        sl = slice(g * LANES, (g + 1) * LANES)
        for src, dst, mult in ((q_ref, qo_ref, scale), (k_ref, ko_ref, None)):
            x = src[:, sl]
            if rope:
                partner = jnp.where(first_half, pltpu.roll(x, LANES - 32, 1), pltpu.roll(x, 32, 1))
                x = x * cos + partner * sin
            if mult is not None:
                x = x * mult
            dst[:, sl] = x.astype(dst.dtype)
    vo_ref[...] = v_ref[...].astype(vo_ref.dtype)


def qkv_prep(proj, cos, sin, seq, scale, rope):
    t = proj.shape[0]
    tr = _tile(seq, 512)
    col = lambda c: pl.BlockSpec((tr, BRANCH_COLS), lambda i: (i, c))
    tab = pl.BlockSpec((tr, LANES), lambda i: (i % (seq // tr), 0))
    out = jax.ShapeDtypeStruct((t, BRANCH_COLS), BF16)
    ospec = pl.BlockSpec((tr, BRANCH_COLS), lambda i: (i, 0))
    return pl.pallas_call(
        functools.partial(_qkv_prep_kernel, scale=scale, rope=rope),
        grid=(t // tr,),
        in_specs=[col(COL_DA_Q), col(COL_DA_K), col(COL_DA_V), tab, tab],
        out_specs=[ospec, ospec, ospec],
        out_shape=[out, out, out],
        compiler_params=_cparams("arbitrary"),
        name="qkv_prep",
    )(proj, proj, proj, cos, sin)


def _attn_kernel(*refs, has_latent):
    if has_latent:
        q_ref, kc_ref, vc_ref, kl_ref, vl_ref, par_ref, o_ref = refs
    else:
        q_ref, kc_ref, vc_ref, par_ref, o_ref = refs
    q = q_ref[...]
    lane = lax.broadcasted_iota(jnp.int32, q.shape, 1)
    outs = []
    for j in range(2):
        qj = jnp.where((lane // 64) == j, q, jnp.zeros_like(q))
        sc = _dot_nt(qj, kc_ref[...])
        m = jnp.max(sc, axis=-1, keepdims=True)
        if has_latent:
            sl = _dot_nt(qj, kl_ref[...])
            m = jnp.maximum(m, jnp.max(sl, axis=-1, keepdims=True))
        pc = jnp.exp2(sc - m)
        den = jnp.sum(pc, axis=-1, keepdims=True)
        acc = _dot(pc.astype(BF16), vc_ref[...])
        if has_latent:
            pl_ = jnp.exp2(sl - m)
            den = den + jnp.sum(pl_, axis=-1, keepdims=True)
            acc = acc + _dot(pl_.astype(BF16), vl_ref[...])
        outs.append(acc / den)
    o = outs[0] - par_ref[0:1, :] * outs[1]
    y = o * lax.rsqrt(jnp.mean(o * o, axis=-1, keepdims=True) + EPS) * par_ref[1:2, :]
    o_ref[...] = (y * par_ref[2:3, :]).astype(o_ref.dtype)


def diff_attention(q, kc, vc, kl, vl, par, bsz, n_heads):
    t, width = q.shape
    hw = width // n_heads
    seq = t // bsz
    n_ctx = kc.shape[0] // bsz
    has_latent = kl is not None
    tq = _tile(seq, 256)
    nq = seq // tq
    in_specs = [pl.BlockSpec((tq, hw), lambda b, h, i: (b * nq + i, h)),
                pl.BlockSpec((n_ctx, hw), lambda b, h, i: (b, h)),
                pl.BlockSpec((n_ctx, hw), lambda b, h, i: (b, h))]
    args = [q, kc, vc]
    if has_latent:
        s = kl.shape[0] // bsz
        in_specs += [pl.BlockSpec((s, hw), lambda b, h, i: (b, h))] * 2
        args += [kl, vl]
    in_specs.append(pl.BlockSpec((SUBLANES, hw), lambda b, h, i: (0, 0)))
    args.append(par)
    return pl.pallas_call(
        functools.partial(_attn_kernel, has_latent=has_latent),
        grid=(bsz, n_heads, nq),
        in_specs=in_specs,
        out_specs=pl.BlockSpec((tq, hw), lambda b, h, i: (b * nq + i, h)),
        out_shape=jax.ShapeDtypeStruct((t, width), BF16),
        compiler_params=_cparams("arbitrary", "arbitrary", "arbitrary"),
        name="diff_attention",
    )(*args)


def _hgrn_kernel(ql_ref, zfl_ref, zbl_ref, vl_ref, gl_ref, qc_ref, zfc_ref, zbc_ref, vc_ref, gc_ref,
                 lb_ref, on_ref, yl_ref, yc_ref, ol_ref, oc_ref, st_ref):
    c = HG_CHUNK
    row = lax.broadcasted_iota(jnp.int32, (c, LANES), 0)
    ti = lax.broadcasted_iota(jnp.int32, (c, c), 0)
    si = lax.broadcasted_iota(jnp.int32, (c, c), 1)

    def chunk(q, z, v, d, reverse):
        lb = lb_ref[d:d + 1, :]
        f = lb + (1.0 - lb) * _sigmoid(z)
        kk = (1.0 - lb) * _sigmoid(-z)
        cum = jnp.log(jnp.maximum(f, LOGF_FLOOR))
        for s in (1, 2, 4, 8, 16, 32):
            if reverse:
                cum = cum + jnp.where(row < c - s, pltpu.roll(cum, c - s, 0), 0.0)
            else:
                cum = cum + jnp.where(row >= s, pltpu.roll(cum, s, 0), 0.0)
        mid = cum[c // 2:c // 2 + 1, :]
        c_end = cum[0:1, :] if reverse else cum[c - 1:c, :]
        qt = (q * jnp.exp(cum - mid)).astype(BF16)
        kt = (kk * jnp.exp(mid - cum)).astype(BF16)
        att = _dot_nt(qt, kt)
        att = jnp.where((ti <= si) if reverse else (ti >= si), att, 0.0)
        vb = v.astype(BF16)
        state = st_ref[d]
        o = _dot(att.astype(BF16), vb) + _dot_nt((q * jnp.exp(cum)).astype(BF16), state.astype(BF16))
        kd = (kk * jnp.exp(c_end - cum)).astype(BF16)
        st_ref[d] = state * jnp.exp(c_end) + _dot_tn(vb, kd)
        return o

    def segment(q_ref, zf_ref, zb_ref, v_ref, o_ref):
        n = q_ref.shape[0] // c

        def body(i, _):
            r0 = pl.multiple_of(i * c, c)
            o_ref[0, pl.ds(r0, c), :] = chunk(q_ref[pl.ds(r0, c), :], zf_ref[pl.ds(r0, c), :],
                                              v_ref[pl.ds(r0, c), :], 0, False)
            r1 = pl.multiple_of((n - 1 - i) * c, c)
            o_ref[1, pl.ds(r1, c), :] = chunk(q_ref[pl.ds(r1, c), :], zb_ref[pl.ds(r1, c), :],
                                              v_ref[pl.ds(r1, c), :], 1, True)
            return 0

        lax.fori_loop(0, n, body, 0, unroll=4)

    def finish(o_ref, g_ref, y_ref):
        o = o_ref[0] + o_ref[1]
        y = o * lax.rsqrt(jnp.mean(o * o, axis=-1, keepdims=True) + EPS) * on_ref[...]
        g = g_ref[...]
        y_ref[...] = (y * (g * _sigmoid(g))).astype(y_ref.dtype)

    st_ref[...] = jnp.zeros_like(st_ref)
    segment(qc_ref, zfc_ref, zbc_ref, vc_ref, oc_ref)
    segment(ql_ref, zfl_ref, zbl_ref, vl_ref, ol_ref)
    finish(oc_ref, gc_ref, yc_ref)
    finish(ol_ref, gl_ref, yl_ref)


def hgrn(proj_l, proj_c, bsz, lb, onorm_g):
    s = proj_l.shape[0] // bsz
    n_ctx = proj_c.shape[0] // bsz
    dv = onorm_g.shape[0]
    assert dv == LANES and s % HG_CHUNK == 0 and n_ctx % HG_CHUNK == 0
    n_heads = BRANCH_COLS // dv
    row_l = lambda col: pl.BlockSpec((s, dv), lambda b, h: (b, col * n_heads + h))
    row_c = lambda col: pl.BlockSpec((n_ctx, dv), lambda b, h: (b, col * n_heads + h))
    cols = (COL_HG_Q, COL_HG_FF, COL_HG_FB, COL_HG_I, COL_HG_G)
    return pl.pallas_call(
        _hgrn_kernel,
        grid=(bsz, n_heads),
        in_specs=[row_l(cc) for cc in cols] + [row_c(cc) for cc in cols]
        + [pl.BlockSpec((2, dv), lambda b, h: (0, h)), pl.BlockSpec((1, dv), lambda b, h: (0, 0))],
        out_specs=[pl.BlockSpec((s, dv), lambda b, h: (b, h)),
                   pl.BlockSpec((n_ctx, dv), lambda b, h: (b, h))],
        out_shape=[jax.ShapeDtypeStruct((bsz * s, BRANCH_COLS), BF16),
                   jax.ShapeDtypeStruct((bsz * n_ctx, BRANCH_COLS), BF16)],
        scratch_shapes=[pltpu.VMEM((2, s, dv), F32), pltpu.VMEM((2, n_ctx, dv), F32),
                        pltpu.VMEM((2, dv, dv), F32)],
        compiler_params=_cparams("arbitrary", "arbitrary"),
        name="hgrn",
    )(*([proj_l] * 5 + [proj_c] * 5 + [lb, onorm_g.reshape(1, dv)]))


def _merge_kernel(y0_ref, y1_ref, y2_ref, g0_ref, g1_ref, g2_ref, w_ref, bg_ref, o_ref):
    acc = None
    for kb, (y_ref, g_ref) in enumerate(((y0_ref, g0_ref), (y1_ref, g1_ref), (y2_ref, g2_ref))):
        gate = _sigmoid(g_ref[...] + bg_ref[kb:kb + 1, :])
        term = gate * _dot(y_ref[...], w_ref[kb].astype(BF16))
        acc = term if acc is None else acc + term
    o_ref[...] = acc.astype(o_ref.dtype)


def merge(ys, proj, w_branch, layer, b_gate):
    t, k = ys[0].shape
    _, nb, _, d = w_branch.shape
    tm, tn = _tile(t, 1024), _tile(d, 512)
    gate_col0 = COL_GATES * BRANCH_COLS // tn
    yspec = pl.BlockSpec((tm, k), lambda i, j: (i, 0))
    gspec = lambda kb: pl.BlockSpec((tm, tn), lambda i, j: (i, gate_col0 + kb * (d // tn) + j))
    return pl.pallas_call(
        _merge_kernel,
        grid=(t // tm, d // tn),
        in_specs=[yspec, yspec, yspec, gspec(0), gspec(1), gspec(2),
                  pl.BlockSpec((None, nb, k, tn), lambda i, j: (layer, 0, 0, j)),
                  pl.BlockSpec((nb, tn), lambda i, j: (0, j))],
        out_specs=pl.BlockSpec((tm, tn), lambda i, j: (i, j)),
        out_shape=jax.ShapeDtypeStruct((t, d), BF16),
        compiler_params=_cparams("arbitrary", "arbitrary"),
        name="merge",
    )(ys[0], ys[1], ys[2], proj, proj, proj, w_branch, b_gate)


def _peer_tables_kernel(q_ref, sub_ref, c1_ref, e1_ref, r2_ref, e2_ref, top_ref, s_ref, work_ref):
    _, n_heads, n_keys, tt = s_ref.shape
    neg_inf = jnp.float32(-jnp.inf)
    for j in range(2):
        sub = sub_ref[j].astype(BF16)
        for h in range(n_heads):
            c0 = (h * 2 + j) * n_keys
            s = _dot_nt(sub, q_ref[:, c0:c0 + n_keys].astype(BF16))
            s_ref[j, h] = s
            work_ref[h] = s

        def extract(i, _, j=j):
            for h in range(n_heads):
                work = work_ref[h]
                m = jnp.max(work, axis=0, keepdims=True)
                top_ref[j, i, pl.ds(h, 1), :] = m
                work_ref[h] = jnp.where(work == m, neg_inf, work)
            return 0

        lax.fori_loop(0, PEER_TOPK, extract, 0)
    a = [top_ref[0, i] for i in range(PEER_TOPK)]
    b = [top_ref[1, i] for i in range(PEER_TOPK)]
    cands = [a[i] + b[jj] for i in range(PEER_TOPK) for jj in range(PEER_TOPK) if (i + 1) * (jj + 1) <= PEER_TOPK]
    tops = []
    for _ in range(PEER_TOPK):
        m = functools.reduce(jnp.maximum, cands)
        tops.append(m)
        taken = None
        nxt = []
        for cand in cands:
            eq = cand == m
            hit = eq if taken is None else jnp.logical_and(eq, jnp.logical_not(taken))
            nxt.append(jnp.where(hit, neg_inf, cand))
            taken = eq if taken is None else jnp.logical_or(taken, eq)
        cands = nxt
    tau = tops[-1]
    z = functools.reduce(lambda p, q: p + q, [jnp.exp(t - tops[0]) for t in tops])
    inv_z = 1.0 / z
    count = []
    for i in range(PEER_TOPK):
        n = None
        for jj in range(PEER_TOPK // (i + 1)):
            hit = jnp.where(a[i] + b[jj] >= tau, 1.0, 0.0)
            n = hit if n is None else n + hit
        count.append(n)
    for h in range(n_heads):
        s1 = s_ref[0, h]
        s2 = s_ref[1, h]
        c1 = jnp.zeros_like(s1)
        rank2 = jnp.zeros_like(s2)
        for i in range(PEER_TOPK):
            c1 = jnp.where(s1 == a[i][h:h + 1, :], count[i][h:h + 1, :], c1)
            rank2 = rank2 + jnp.where(b[i][h:h + 1, :] > s2, 1.0, 0.0)
        c1_ref[h] = c1
        r2_ref[h] = rank2.astype(r2_ref.dtype)
        e1_ref[h] = jnp.exp(s1 - a[0][h:h + 1, :]) * inv_z[h:h + 1, :]
        e2_ref[h] = jnp.exp(s2 - b[0][h:h + 1, :]).astype(e2_ref.dtype)


def peer_tables(q, subkeys, n_heads):
    t = q.shape[0]
    n_keys = subkeys.shape[1]
    assert n_keys == LANES and n_heads == SUBLANES
    tt = _tile(t, 256)
    tspec = pl.BlockSpec((n_heads, n_keys, tt), lambda i: (0, 0, i))
    tab = lambda dt: jax.ShapeDtypeStruct((n_heads, n_keys, t), dt)
    return pl.pallas_call(
        _peer_tables_kernel,
        grid=(t // tt,),
        in_specs=[pl.BlockSpec((tt, q.shape[1]), lambda i: (i, 0)),
                  pl.BlockSpec(subkeys.shape, lambda i: (0, 0, 0))],
        out_specs=[tspec, tspec, tspec, tspec],
        out_shape=[tab(F32), tab(F32), tab(BF16), tab(BF16)],
        scratch_shapes=[pltpu.VMEM((2, PEER_TOPK, n_heads, tt), F32),
                        pltpu.VMEM((2, n_heads, n_keys, tt), F32),
                        pltpu.VMEM((n_heads, n_keys, tt), F32)],
        compiler_params=_cparams("arbitrary"),
        name="peer_tables",
    )(q, subkeys)


def _peer_main_kernel(h_ref, u_ref, v_ref, c1_ref, e1_ref, r2_ref, e2_ref, res_ref, gate_ref, o_ref, w_ref):
    e = pl.program_id(1)
    groups, n_heads, tm = c1_ref.shape
    n_keys = r2_ref.shape[1]

    @pl.when(e == 0)
    def _():
        o_ref[...] = jnp.zeros_like(o_ref)

    act = _gelu(_dot_nt(u_ref[...], h_ref[...]))
    for c in range(tm // LANES):
        cs = slice(c * LANES, (c + 1) * LANES)
        for a0 in range(0, groups, 2):
            accs = [None, None]
            for h in range(n_heads):
                r2c = r2_ref[h, :, cs]
                e2c = e2_ref[h, :, cs]
                for k in range(2):
                    c1row = c1_ref[a0 + k, h:h + 1, cs].astype(BF16)
                    e1row = e1_ref[a0 + k, h:h + 1, cs].astype(BF16)
                    term = jnp.where(r2c < c1row, e2c, jnp.zeros_like(e2c)) * e1row
                    accs[k] = term if accs[k] is None else accs[k] + term
            for k in range(2):
                w_ref[(a0 + k) * n_keys:(a0 + k + 1) * n_keys, cs] = accs[k].astype(F32)
    o_ref[...] += _dot((w_ref[...] * act).T.astype(BF16), v_ref[...])

    @pl.when(e == pl.num_programs(1) - 1)
    def _():
        o_ref[...] = res_ref[...] + gate_ref[...] * o_ref[...]


def peer_main(h, u, v, layer, tables, res, gate, rows_per_src):
    c1, e1, r2, e2 = tables
    c1, e1 = jnp.swapaxes(c1, 0, 1), jnp.swapaxes(e1, 0, 1)
    t, d = h.shape
    n_exp = u.shape[1]
    n_heads, n_keys, _ = r2.shape
    tm = _tile(rows_per_src, 512)
    groups = 8
    te = groups * n_keys
    tok = pl.BlockSpec((tm, d), lambda i, e: (i, 0))
    exp = pl.BlockSpec((None, te, d), lambda i, e: (layer, e, 0))
    grp = pl.BlockSpec((groups, n_heads, tm), lambda i, e: (e, 0, i))
    full = pl.BlockSpec((n_heads, n_keys, tm), lambda i, e: (0, 0, i))
    return pl.pallas_call(
        _peer_main_kernel,
        grid=(t // tm, n_exp // te),
        in_specs=[tok, exp, exp, grp, grp, full, full, tok,
                  pl.BlockSpec((None, 1, d), lambda i, e: (i * tm // rows_per_src, 0, 0))],
        out_specs=tok,
        out_shape=jax.ShapeDtypeStruct((t, d), F32),
        scratch_shapes=[pltpu.VMEM((te, tm), F32)],
        compiler_params=_cparams("arbitrary", "arbitrary"),
        name="peer_main",
    )(h, u, v, c1, e1, r2, e2, res, gate)


def _rope_tables(seq, head_dim):
    rows = seq // GRID_W
    axis_dim = head_dim // 2
    r = jnp.repeat(jnp.arange(rows, dtype=F32), GRID_W)
    col = jnp.tile(jnp.arange(GRID_W, dtype=F32), rows)
    inv = ROPE_BASE ** (-jnp.arange(0, axis_dim, 2, dtype=F32) / axis_dim)
    ang = jnp.concatenate([r[:, None] * inv, col[:, None] * inv], axis=-1)
    cos, sin = jnp.cos(ang), jnp.sin(ang)
    reps = LANES // head_dim
    cos_l = jnp.tile(jnp.concatenate([cos, cos], axis=-1), (1, reps))
    sin_l = jnp.tile(jnp.concatenate([-sin, sin], axis=-1), (1, reps))
    return cos_l, sin_l


def kernel(x, c, ctx, c_ctx, w_ada, b_ada, norm_mix_g, norm_ffn_g, w_in, rg_conv_w, rg_conv_b, rg_wa, rg_ba, rg_wx, rg_bx, rg_lambda, da_lq, da_lk, da_subln_g, hg_lb, hg_onorm_g, w_branch, b_gate, w_out, peer_wq, peer_subkeys, peer_u, peer_v, final_norm_g):
    bsz, s, d = x.shape
    n_ctx = ctx.shape[1]
    depth = w_in.shape[0]
    head_dim = da_lq.shape[-1]
    da_heads = BRANCH_COLS // (2 * head_dim)
    peer_heads = peer_wq.shape[-1] // (2 * peer_subkeys.shape[-1])
    assert bsz + 1 <= SUBLANES and 2 * head_dim == LANES

    xl = x.reshape(bsz * s, d)
    xc = ctx.reshape(bsz * n_ctx, d)
    cvec = jnp.concatenate([c, c_ctx[None, :], jnp.zeros((SUBLANES - bsz - 1, d), F32)], axis=0)
    mods = ada_mods(cvec, w_ada, b_ada).reshape(depth, SUBLANES, N_MOD, d)
    lb = jnp.cumsum(jax.nn.softmax(hg_lb.astype(F32), axis=1), axis=1)
    lb = lb - lb[:, :1]
    cos_l, sin_l = _rope_tables(s, head_dim)
    scale = head_dim ** -0.5 * math.log2(math.e)
    u_bf = peer_u.astype(BF16)
    v_bf = peer_v.astype(BF16)

    for l in range(depth):
        last = l == depth - 1
        ml = [mods[l, :bsz, k][:, None, :] for k in range(N_MOD)]
        mc = [mods[l, bsz:bsz + 1, k][:, None, :] for k in range(N_MOD)]
        lam_init = 0.8 - 0.6 * math.exp(-0.3 * l)
        lam = jnp.exp(jnp.sum(da_lq[l, 0] * da_lk[l, 0])) - jnp.exp(jnp.sum(da_lq[l, 1] * da_lk[l, 1])) + lam_init
        par = jnp.concatenate([jnp.full((1, LANES), lam, F32), da_subln_g[l][None, :],
                               jnp.full((1, LANES), 1.0 - lam_init, F32),
                               jnp.zeros((SUBLANES - 3, LANES), F32)], axis=0)

        hl = norm_mod(xl, norm_mix_g[l], ml[0], ml[1], s)
        hc = norm_mod(xc, norm_mix_g[l], mc[0], mc[1], bsz * n_ctx)
        pj_l = matmul(hl, w_in, l, F32, tn_pref=1024)
        pj_c = matmul(hc, w_in, l, F32, tn_pref=1024)
        rg_l, rg_c = rglru(pj_l, pj_c, bsz, rg_conv_w[l], rg_conv_b[l], rg_wa[l], rg_ba[l], rg_wx[l], rg_bx[l],
                           rg_lambda[l])
        ql, kl, vl = qkv_prep(pj_l, cos_l, sin_l, s, scale, True)
        qc, kc, vc = qkv_prep(pj_c, cos_l, sin_l, n_ctx, scale, False)
        da_l = diff_attention(ql, kc, vc, kl, vl, par, bsz, da_heads)
        hg_l, hg_c = hgrn(pj_l, pj_c, bsz, lb[:, l], hg_onorm_g[l])
        mg_l = merge((rg_l, da_l, hg_l), pj_l, w_branch, l, b_gate[l])
        xl = matmul_residual(mg_l, w_out, l, xl, ml[2], s)
        if not last:
            da_c = diff_attention(qc, kc, vc, None, None, par, bsz, da_heads)
            mg_c = merge((rg_c, da_c, hg_c), pj_c, w_branch, l, b_gate[l])
            xc = matmul_residual(mg_c, w_out, l, xc, mc[2], bsz * n_ctx)

        hl = norm_mod(xl, norm_ffn_g[l], ml[3], ml[4], s)
        tabs = peer_tables(matmul(hl, peer_wq, l, F32), peer_subkeys[l], peer_heads)
        xl = peer_main(hl, u_bf, v_bf, l, tabs, xl, ml[5], s)
        if not last:
            hc = norm_mod(xc, norm_ffn_g[l], mc[3], mc[4], bsz * n_ctx)
            tabs = peer_tables(matmul(hc, peer_wq, l, F32), peer_subkeys[l], peer_heads)
            xc = peer_main(hc, u_bf, v_bf, l, tabs, xc, mc[5], bsz * n_ctx)

    return plain_norm(xl, final_norm_g).reshape(bsz, s, d)
```

```python
import functools
import math

import jax
import jax.numpy as jnp
from jax import lax
from jax.experimental import pallas as pl
from jax.experimental.pallas import tpu as pltpu

F32 = jnp.float32
BF16 = jnp.bfloat16

N_MOD = 6
EPS = 1e-6
GRID_W = 64
ROPE_BASE = 10000.0
RG_C = 8.0
LOGF_FLOOR = 1e-20
HG_CHUNK = 64
PEER_TOPK = 16
BRANCH_COLS = 1024
LANES = 128
SUBLANES = 8
VMEM_LIMIT_BYTES = 56 * 1024 * 1024

COL_RG_X, COL_RG_GATE, COL_DA_Q, COL_DA_K, COL_DA_V, COL_HG_Q, COL_HG_FF, COL_HG_FB, COL_HG_I, COL_HG_G, COL_GATES = range(11)


def _cparams(*sem):
    return pltpu.CompilerParams(dimension_semantics=sem, vmem_limit_bytes=VMEM_LIMIT_BYTES)


def _tile(n, pref):
    t = min(n, pref)
    assert n % t == 0, (n, pref)
    return t


def _gelu(x):
    return 0.5 * x * (1.0 + jnp.tanh(math.sqrt(2.0 / math.pi) * (x + 0.044715 * (x * x * x))))


def _sigmoid(x):
    return 1.0 / (1.0 + jnp.exp(-x))


def _dot(a, b):
    return jnp.dot(a, b, preferred_element_type=F32)


def _dot_nt(a, b):
    return lax.dot_general(a, b, (((1,), (1,)), ((), ())), preferred_element_type=F32)


def _dot_tn(a, b):
    return lax.dot_general(a, b, (((0,), (0,)), ((), ())), preferred_element_type=F32)


def _ada_kernel(c_ref, w_ref, b_ref, o_ref):
    c = c_ref[...]
    sc = c * _sigmoid(c)
    o_ref[...] = _dot(sc.astype(BF16), w_ref[...].astype(BF16)) + b_ref[...]


def ada_mods(cvec, w_ada, b_ada):
    depth, d, n = w_ada.shape
    tn = _tile(n, 1024)
    return pl.pallas_call(
        _ada_kernel,
        grid=(depth, n // tn),
        in_specs=[pl.BlockSpec((SUBLANES, d), lambda l, j: (0, 0)),
                  pl.BlockSpec((None, d, tn), lambda l, j: (l, 0, j)),
                  pl.BlockSpec((None, 1, tn), lambda l, j: (l, 0, j))],
        out_specs=pl.BlockSpec((None, SUBLANES, tn), lambda l, j: (l, 0, j)),
        out_shape=jax.ShapeDtypeStruct((depth, SUBLANES, n), F32),
        compiler_params=_cparams("arbitrary", "arbitrary"),
        name="ada_mods",
    )(cvec, w_ada, b_ada.reshape(depth, 1, n))


def _norm_kernel(x_ref, g_ref, sh_ref, sc_ref, o_ref):
    x = x_ref[...]
    y = x * lax.rsqrt(jnp.mean(x * x, axis=-1, keepdims=True) + EPS) * g_ref[...]
    o_ref[...] = (y * (1.0 + sc_ref[...]) + sh_ref[...]).astype(o_ref.dtype)


def _plain_norm_kernel(x_ref, g_ref, o_ref):
    x = x_ref[...]
    o_ref[...] = (x * lax.rsqrt(jnp.mean(x * x, axis=-1, keepdims=True) + EPS) * g_ref[...]).astype(o_ref.dtype)


def norm_mod(x, g, shift, scale, rows_per_src):
    t, d = x.shape
    tr = _tile(rows_per_src, 512)
    src = lambda i: (i * tr // rows_per_src, 0, 0)
    return pl.pallas_call(
        _norm_kernel,
        grid=(t // tr,),
        in_specs=[pl.BlockSpec((tr, d), lambda i: (i, 0)),
                  pl.BlockSpec((1, d), lambda i: (0, 0)),
                  pl.BlockSpec((None, 1, d), src),
                  pl.BlockSpec((None, 1, d), src)],
        out_specs=pl.BlockSpec((tr, d), lambda i: (i, 0)),
        out_shape=jax.ShapeDtypeStruct((t, d), BF16),
        compiler_params=_cparams("arbitrary"),
        name="norm_mod",
    )(x, g.reshape(1, d), shift, scale)


def plain_norm(x, g):
    t, d = x.shape
    tr = _tile(t, 512)
    return pl.pallas_call(
        _plain_norm_kernel,
        grid=(t // tr,),
        in_specs=[pl.BlockSpec((tr, d), lambda i: (i, 0)),
                  pl.BlockSpec((1, d), lambda i: (0, 0))],
        out_specs=pl.BlockSpec((tr, d), lambda i: (i, 0)),
        out_shape=jax.ShapeDtypeStruct((t, d), x.dtype),
        compiler_params=_cparams("arbitrary"),
        name="final_norm",
    )(x, g.reshape(1, d))


def _mm_kernel(a_ref, w_ref, o_ref):
    o_ref[...] = _dot(a_ref[...], w_ref[...].astype(BF16)).astype(o_ref.dtype)


def matmul(a, w, layer, out_dtype, tm_pref=1024, tn_pref=512):
    m, k = a.shape
    n = w.shape[2]
    tm, tn = _tile(m, tm_pref), _tile(n, tn_pref)
    return pl.pallas_call(
        _mm_kernel,
        grid=(m // tm, n // tn),
        in_specs=[pl.BlockSpec((tm, k), lambda i, j: (i, 0)),
                  pl.BlockSpec((None, k, tn), lambda i, j: (layer, 0, j))],
        out_specs=pl.BlockSpec((tm, tn), lambda i, j: (i, j)),
        out_shape=jax.ShapeDtypeStruct((m, n), out_dtype),
        compiler_params=_cparams("arbitrary", "arbitrary"),
        name="matmul",
    )(a, w)


def _mm_res_kernel(a_ref, w_ref, res_ref, gate_ref, o_ref):
    o_ref[...] = res_ref[...] + gate_ref[...] * _dot(a_ref[...], w_ref[...].astype(BF16))


def matmul_residual(a, w, layer, res, gate, rows_per_src):
    m, k = a.shape
    n = w.shape[2]
    tm, tn = _tile(rows_per_src, 1024), _tile(n, 512)
    return pl.pallas_call(
        _mm_res_kernel,
        grid=(m // tm, n // tn),
        in_specs=[pl.BlockSpec((tm, k), lambda i, j: (i, 0)),
                  pl.BlockSpec((None, k, tn), lambda i, j: (layer, 0, j)),
                  pl.BlockSpec((tm, tn), lambda i, j: (i, j)),
                  pl.BlockSpec((None, 1, tn), lambda i, j: (i * tm // rows_per_src, 0, j))],
        out_specs=pl.BlockSpec((tm, tn), lambda i, j: (i, j)),
        out_shape=jax.ShapeDtypeStruct((m, n), F32),
        compiler_params=_cparams("arbitrary", "arbitrary"),
        name="matmul_residual",
    )(a, w, res, gate)


def _rglru_kernel(xl_ref, gl_ref, xc_ref, gc_ref, cw_ref, cb_ref, wa_ref, ba_ref, wx_ref, bx_ref, lam_ref,
                  yl_ref, yc_ref, al_ref, bl_ref, ac_ref, bc_ref, hl_ref, hc_ref):
    cw = cw_ref[...]
    cb = cb_ref[...]

    def coeffs(x_ref, a_ref, b_ref):
        n = x_ref.shape[0]
        x = x_ref[...]
        row = lax.broadcasted_iota(jnp.int32, x.shape, 0)
        xm2 = jnp.where(row >= 2, pltpu.roll(x, 2, 0), 0.0)
        xm1 = jnp.where(row >= 1, pltpu.roll(x, 1, 0), 0.0)
        xp1 = jnp.where(row < n - 1, pltpu.roll(x, n - 1, 0), 0.0)
        u = cb + cw[0:1] * xm2
        u = u + cw[1:2] * xm1
        u = u + cw[2:3] * x
        u = u + cw[3:4] * xp1
        ub = u.astype(BF16)
        for d in range(2):
            r = _sigmoid(_dot(ub, wa_ref[d].astype(BF16)) + ba_ref[d:d + 1, :])
            i = _sigmoid(_dot(ub, wx_ref[d].astype(BF16)) + bx_ref[d:d + 1, :])
            lam = lam_ref[d:d + 1, :]
            softplus_neg_lam = jnp.maximum(-lam, 0.0) + jnp.log(1.0 + jnp.exp(-jnp.abs(lam)))
            log_a = -RG_C * r * softplus_neg_lam
            a_ref[d] = jnp.exp(log_a)
            th = jnp.tanh(log_a)
            b_ref[d] = jnp.sqrt(-2.0 * th / (1.0 - th)) * (i * u)

    row8 = lax.broadcasted_iota(jnp.int32, (SUBLANES, LANES), 0)

    group = 4

    def local_scan(a, b, reverse):
        for s in (1, 2, 4):
            if reverse:
                m = row8 < SUBLANES - s
                b = jnp.where(m, a * pltpu.roll(b, SUBLANES - s, 0) + b, b)
                a = jnp.where(m, a * pltpu.roll(a, SUBLANES - s, 0), a)
            else:
                m = row8 >= s
                b = jnp.where(m, a * pltpu.roll(b, s, 0) + b, b)
                a = jnp.where(m, a * pltpu.roll(a, s, 0), a)
        return a, b

    def scan(a_ref, b_ref, h_ref, carry):
        rows = group * SUBLANES
        n_it = a_ref.shape[1] // rows

        def body(i, carry):
            cf, cbk = carry
            r0 = pl.multiple_of(i * rows, rows)
            r1 = pl.multiple_of((n_it - 1 - i) * rows, rows)
            fwd = [local_scan(a_ref[0, pl.ds(r0 + v * SUBLANES, SUBLANES), :],
                              b_ref[0, pl.ds(r0 + v * SUBLANES, SUBLANES), :], False) for v in range(group)]
            bwd = [local_scan(a_ref[1, pl.ds(r1 + v * SUBLANES, SUBLANES), :],
                              b_ref[1, pl.ds(r1 + v * SUBLANES, SUBLANES), :], True) for v in range(group)]
            for v in range(group):
                a, b = fwd[v]
                h = b + a * cf
                h_ref[0, pl.ds(r0 + v * SUBLANES, SUBLANES), :] = h
                cf = h[SUBLANES - 1:SUBLANES, :]
                a, b = bwd[group - 1 - v]
                h = b + a * cbk
                h_ref[1, pl.ds(r1 + (group - 1 - v) * SUBLANES, SUBLANES), :] = h
                cbk = h[0:1, :]
            return cf, cbk

        return lax.fori_loop(0, n_it, body, carry)

    coeffs(xc_ref, ac_ref, bc_ref)
    coeffs(xl_ref, al_ref, bl_ref)
    zero = jnp.zeros((1, LANES), F32)
    carry = scan(ac_ref, bc_ref, hc_ref, (zero, zero))
    scan(al_ref, bl_ref, hl_ref, carry)
    yc_ref[...] = ((hc_ref[0] + hc_ref[1]) * _gelu(gc_ref[...])).astype(yc_ref.dtype)
    yl_ref[...] = ((hl_ref[0] + hl_ref[1]) * _gelu(gl_ref[...])).astype(yl_ref.dtype)


def rglru(proj_l, proj_c, bsz, conv_w, conv_b, wa, ba, wx, bx, lam):
    s = proj_l.shape[0] // bsz
    n_ctx = proj_c.shape[0] // bsz
    nblk, bd = wa.shape[1], wa.shape[2]
    assert bd == LANES
    width = nblk * bd
    cpb = BRANCH_COLS // bd
    row_l = lambda col: pl.BlockSpec((s, bd), lambda b, j: (b, col * cpb + j))
    row_c = lambda col: pl.BlockSpec((n_ctx, bd), lambda b, j: (b, col * cpb + j))
    vec = lambda r: pl.BlockSpec((r, bd), lambda b, j: (0, j))
    wspec = pl.BlockSpec((2, None, bd, bd), lambda b, j: (0, j, 0, 0))
    return pl.pallas_call(
        _rglru_kernel,
        grid=(bsz, nblk),
        in_specs=[row_l(COL_RG_X), row_l(COL_RG_GATE), row_c(COL_RG_X), row_c(COL_RG_GATE),
                  vec(conv_w.shape[0]), vec(1), wspec, vec(2), wspec, vec(2), vec(2)],
        out_specs=[pl.BlockSpec((s, bd), lambda b, j: (b, j)),
                   pl.BlockSpec((n_ctx, bd), lambda b, j: (b, j))],
        out_shape=[jax.ShapeDtypeStruct((bsz * s, width), BF16),
                   jax.ShapeDtypeStruct((bsz * n_ctx, width), BF16)],
        scratch_shapes=[pltpu.VMEM((2, s, bd), F32), pltpu.VMEM((2, s, bd), F32),
                        pltpu.VMEM((2, n_ctx, bd), F32), pltpu.VMEM((2, n_ctx, bd), F32),
                        pltpu.VMEM((2, s, bd), F32), pltpu.VMEM((2, n_ctx, bd), F32)],
        compiler_params=_cparams("arbitrary", "arbitrary"),
        name="rglru",
    )(proj_l, proj_l, proj_c, proj_c, conv_w, conv_b.reshape(1, width), wa, ba, wx, bx, lam)


def _qkv_prep_kernel(q_ref, k_ref, v_ref, cos_ref, sin_ref, qo_ref, ko_ref, vo_ref, *, scale, rope):
    tr = q_ref.shape[0]
    lane = lax.broadcasted_iota(jnp.int32, (tr, LANES), 1)
    first_half = (lane % 64) < 32
    cos = cos_ref[...]
    sin = sin_ref[...]
    for g in range(q_ref.shape[1] // LANES):
        sl = slice(g * LANES, (g + 1) * LANES)
        for src, dst, mult in ((q_ref, qo_ref, scale), (k_ref, ko_ref, None)):
            x = src[:, sl]
            if rope:
                partner = jnp.where(first_half, pltpu.roll(x, LANES - 32, 1), pltpu.roll(x, 32, 1))
                x = x * cos + partner * sin
            if mult is not None:
                x = x * mult
            dst[:, sl] = x.astype(dst.dtype)
    vo_ref[...] = v_ref[...].astype(vo_ref.dtype)


def qkv_prep(proj, cos, sin, seq, scale, rope):
    t = proj.shape[0]
    tr = _tile(seq, 512)
    col = lambda c: pl.BlockSpec((tr, BRANCH_COLS), lambda i: (i, c))
    tab = pl.BlockSpec((tr, LANES), lambda i: (i % (seq // tr), 0))
    out = jax.ShapeDtypeStruct((t, BRANCH_COLS), BF16)
    ospec = pl.BlockSpec((tr, BRANCH_COLS), lambda i: (i, 0))
    return pl.pallas_call(
        functools.partial(_qkv_prep_kernel, scale=scale, rope=rope),
        grid=(t // tr,),
        in_specs=[col(COL_DA_Q), col(COL_DA_K), col(COL_DA_V), tab, tab],
        out_specs=[ospec, ospec, ospec],
        out_shape=[out, out, out],
        compiler_params=_cparams("arbitrary"),
        name="qkv_prep",
    )(proj, proj, proj, cos, sin)


def _attn_kernel(*refs, has_latent):
    if has_latent:
        q_ref, kc_ref, vc_ref, kl_ref, vl_ref, par_ref, o_ref = refs
    else:
        q_ref, kc_ref, vc_ref, par_ref, o_ref = refs
    q = q_ref[...]
    lane = lax.broadcasted_iota(jnp.int32, q.shape, 1)
    outs = []
    for j in range(2):
        qj = jnp.where((lane // 64) == j, q, jnp.zeros_like(q))
        sc = _dot_nt(qj, kc_ref[...])
        m = jnp.max(sc, axis=-1, keepdims=True)
        if has_latent:
            sl = _dot_nt(qj, kl_ref[...])
            m = jnp.maximum(m, jnp.max(sl, axis=-1, keepdims=True))
        pc = jnp.exp2(sc - m)
        den = jnp.sum(pc, axis=-1, keepdims=True)
        acc = _dot(pc.astype(BF16), vc_ref[...])
        if has_latent:
            pl_ = jnp.exp2(sl - m)
            den = den + jnp.sum(pl_, axis=-1, keepdims=True)
            acc = acc + _dot(pl_.astype(BF16), vl_ref[...])
        outs.append(acc / den)
    o = outs[0] - par_ref[0:1, :] * outs[1]
    y = o * lax.rsqrt(jnp.mean(o * o, axis=-1, keepdims=True) + EPS) * par_ref[1:2, :]
    o_ref[...] = (y * par_ref[2:3, :]).astype(o_ref.dtype)


def diff_attention(q, kc, vc, kl, vl, par, bsz, n_heads):
    t, width = q.shape
    hw = width // n_heads
    seq = t // bsz
    n_ctx = kc.shape[0] // bsz
    has_latent = kl is not None
    tq = _tile(seq, 256)
    nq = seq // tq
    in_specs = [pl.BlockSpec((tq, hw), lambda b, h, i: (b * nq + i, h)),
                pl.BlockSpec((n_ctx, hw), lambda b, h, i: (b, h)),
                pl.BlockSpec((n_ctx, hw), lambda b, h, i: (b, h))]
    args = [q, kc, vc]
    if has_latent:
        s = kl.shape[0] // bsz
        in_specs += [pl.BlockSpec((s, hw), lambda b, h, i: (b, h))] * 2
        args += [kl, vl]
    in_specs.append(pl.BlockSpec((SUBLANES, hw), lambda b, h, i: (0, 0)))
    args.append(par)
    return pl.pallas_call(
        functools.partial(_attn_kernel, has_latent=has_latent),
        grid=(bsz, n_heads, nq),
        in_specs=in_specs,
        out_specs=pl.BlockSpec((tq, hw), lambda b, h, i: (b * nq + i, h)),
        out_shape=jax.ShapeDtypeStruct((t, width), BF16),
        compiler_params=_cparams("arbitrary", "arbitrary", "arbitrary"),
        name="diff_attention",
    )(*args)


def _hgrn_kernel(ql_ref, zfl_ref, zbl_ref, vl_ref, gl_ref, qc_ref, zfc_ref, zbc_ref, vc_ref, gc_ref,
                 lb_ref, on_ref, yl_ref, yc_ref, ol_ref, oc_ref, st_ref):
    c = HG_CHUNK
    row = lax.broadcasted_iota(jnp.int32, (c, LANES), 0)
    ti = lax.broadcasted_iota(jnp.int32, (c, c), 0)
    si = lax.broadcasted_iota(jnp.int32, (c, c), 1)

    def chunk(q, z, v, d, reverse):
        lb = lb_ref[d:d + 1, :]
        f = lb + (1.0 - lb) * _sigmoid(z)
        kk = (1.0 - lb) * _sigmoid(-z)
        cum = jnp.log(jnp.maximum(f, LOGF_FLOOR))
        for s in (1, 2, 4, 8, 16, 32):
            if reverse:
                cum = cum + jnp.where(row < c - s, pltpu.roll(cum, c - s, 0), 0.0)
            else:
                cum = cum + jnp.where(row >= s, pltpu.roll(cum, s, 0), 0.0)
        mid = cum[c // 2:c // 2 + 1, :]
        c_end = cum[0:1, :] if reverse else cum[c - 1:c, :]
        qt = (q * jnp.exp(cum - mid)).astype(BF16)
        kt = (kk * jnp.exp(mid - cum)).astype(BF16)
        att = _dot_nt(qt, kt)
        att = jnp.where((ti <= si) if reverse else (ti >= si), att, 0.0)
        vb = v.astype(BF16)
        state = st_ref[d]
        o = _dot(att.astype(BF16), vb) + _dot_nt((q * jnp.exp(cum)).astype(BF16), state.astype(BF16))
        kd = (kk * jnp.exp(c_end - cum)).astype(BF16)
        st_ref[d] = state * jnp.exp(c_end) + _dot_tn(vb, kd)
        return o

    def segment(q_ref, zf_ref, zb_ref, v_ref, o_ref):
        n = q_ref.shape[0] // c

        def body(i, _):
            r0 = pl.multiple_of(i * c, c)
            o_ref[0, pl.ds(r0, c), :] = chunk(q_ref[pl.ds(r0, c), :], zf_ref[pl.ds(r0, c), :],
                                              v_ref[pl.ds(r0, c), :], 0, False)
            r1 = pl.multiple_of((n - 1 - i) * c, c)
            o_ref[1, pl.ds(r1, c), :] = chunk(q_ref[pl.ds(r1, c), :], zb_ref[pl.ds(r1, c), :],
                                              v_ref[pl.ds(r1, c), :], 1, True)
            return 0

        lax.fori_loop(0, n, body, 0, unroll=8 if n % 8 == 0 else 4)

    def finish(o_ref, g_ref, y_ref):
        o = o_ref[0] + o_ref[1]
        y = o * lax.rsqrt(jnp.mean(o * o, axis=-1, keepdims=True) + EPS) * on_ref[...]
        g = g_ref[...]
        y_ref[...] = (y * (g * _sigmoid(g))).astype(y_ref.dtype)

    st_ref[...] = jnp.zeros_like(st_ref)
    segment(qc_ref, zfc_ref, zbc_ref, vc_ref, oc_ref)
    segment(ql_ref, zfl_ref, zbl_ref, vl_ref, ol_ref)
    finish(oc_ref, gc_ref, yc_ref)
    finish(ol_ref, gl_ref, yl_ref)


def hgrn(proj_l, proj_c, bsz, lb, onorm_g):
    s = proj_l.shape[0] // bsz
    n_ctx = proj_c.shape[0] // bsz
    dv = onorm_g.shape[0]
    assert dv == LANES and s % HG_CHUNK == 0 and n_ctx % HG_CHUNK == 0
    n_heads = BRANCH_COLS // dv
    row_l = lambda col: pl.BlockSpec((s, dv), lambda b, h: (b, col * n_heads + h))
    row_c = lambda col: pl.BlockSpec((n_ctx, dv), lambda b, h: (b, col * n_heads + h))
    cols = (COL_HG_Q, COL_HG_FF, COL_HG_FB, COL_HG_I, COL_HG_G)
    return pl.pallas_call(
        _hgrn_kernel,
        grid=(bsz, n_heads),
        in_specs=[row_l(cc) for cc in cols] + [row_c(cc) for cc in cols]
        + [pl.BlockSpec((2, dv), lambda b, h: (0, h)), pl.BlockSpec((1, dv), lambda b, h: (0, 0))],
        out_specs=[pl.BlockSpec((s, dv), lambda b, h: (b, h)),
                   pl.BlockSpec((n_ctx, dv), lambda b, h: (b, h))],
        out_shape=[jax.ShapeDtypeStruct((bsz * s, BRANCH_COLS), BF16),
                   jax.ShapeDtypeStruct((bsz * n_ctx, BRANCH_COLS), BF16)],
        scratch_shapes=[pltpu.VMEM((2, s, dv), F32), pltpu.VMEM((2, n_ctx, dv), F32),
                        pltpu.VMEM((2, dv, dv), F32)],
        compiler_params=_cparams("arbitrary", "arbitrary"),
        name="hgrn",
    )(*([proj_l] * 5 + [proj_c] * 5 + [lb, onorm_g.reshape(1, dv)]))


def _merge_kernel(y0_ref, y1_ref, y2_ref, g0_ref, g1_ref, g2_ref, w_ref, bg_ref, o_ref):
    acc = None
    for kb, (y_ref, g_ref) in enumerate(((y0_ref, g0_ref), (y1_ref, g1_ref), (y2_ref, g2_ref))):
        gate = _sigmoid(g_ref[...] + bg_ref[kb:kb + 1, :])
        term = gate * _dot(y_ref[...], w_ref[kb].astype(BF16))
        acc = term if acc is None else acc + term
    o_ref[...] = acc.astype(o_ref.dtype)


def merge(ys, proj, w_branch, layer, b_gate):
    t, k = ys[0].shape
    _, nb, _, d = w_branch.shape
    tm, tn = _tile(t, 1024), _tile(d, 512)
    gate_col0 = COL_GATES * BRANCH_COLS // tn
    yspec = pl.BlockSpec((tm, k), lambda i, j: (i, 0))
    gspec = lambda kb: pl.BlockSpec((tm, tn), lambda i, j: (i, gate_col0 + kb * (d // tn) + j))
    return pl.pallas_call(
        _merge_kernel,
        grid=(t // tm, d // tn),
        in_specs=[yspec, yspec, yspec, gspec(0), gspec(1), gspec(2),
                  pl.BlockSpec((None, nb, k, tn), lambda i, j: (layer, 0, 0, j)),
                  pl.BlockSpec((nb, tn), lambda i, j: (0, j))],
        out_specs=pl.BlockSpec((tm, tn), lambda i, j: (i, j)),
        out_shape=jax.ShapeDtypeStruct((t, d), BF16),
        compiler_params=_cparams("arbitrary", "arbitrary"),
        name="merge",
    )(ys[0], ys[1], ys[2], proj, proj, proj, w_branch, b_gate)


def _peer_tables_kernel(q_ref, sub_ref, c1_ref, e1_ref, r2_ref, e2_ref, top_ref, s_ref, work_ref):
    _, n_heads, n_keys, tt = s_ref.shape
    neg_inf = jnp.float32(-jnp.inf)
    for j in range(2):
        sub = sub_ref[j].astype(BF16)
        for h in range(n_heads):
            c0 = (h * 2 + j) * n_keys
            s = _dot_nt(sub, q_ref[:, c0:c0 + n_keys].astype(BF16))
            s_ref[j, h] = s
            work_ref[h] = s

        def extract(i, _, j=j):
            for h in range(n_heads):
                work = work_ref[h]
                m = jnp.max(work, axis=0, keepdims=True)
                top_ref[j, i, pl.ds(h, 1), :] = m
                work_ref[h] = jnp.where(work == m, neg_inf, work)
            return 0

        lax.fori_loop(0, PEER_TOPK, extract, 0)
    a = [top_ref[0, i] for i in range(PEER_TOPK)]
    b = [top_ref[1, i] for i in range(PEER_TOPK)]
    cands = [a[i] + b[jj] for i in range(PEER_TOPK) for jj in range(PEER_TOPK) if (i + 1) * (jj + 1) <= PEER_TOPK]
    tops = []
    for _ in range(PEER_TOPK):
        m = functools.reduce(jnp.maximum, cands)
        tops.append(m)
        taken = None
        nxt = []
        for cand in cands:
            eq = cand == m
            hit = eq if taken is None else jnp.logical_and(eq, jnp.logical_not(taken))
            nxt.append(jnp.where(hit, neg_inf, cand))
            taken = eq if taken is None else jnp.logical_or(taken, eq)
        cands = nxt
    tau = tops[-1]
    z = functools.reduce(lambda p, q: p + q, [jnp.exp(t - tops[0]) for t in tops])
    inv_z = 1.0 / z
    count = []
    for i in range(PEER_TOPK):
        n = None
        for jj in range(PEER_TOPK // (i + 1)):
            hit = jnp.where(a[i] + b[jj] >= tau, 1.0, 0.0)
            n = hit if n is None else n + hit
        count.append(n)
    for h in range(n_heads):
        s1 = s_ref[0, h]
        s2 = s_ref[1, h]
        c1 = jnp.zeros_like(s1)
        rank2 = jnp.zeros_like(s2)
        for i in range(PEER_TOPK):
            c1 = jnp.where(s1 == a[i][h:h + 1, :], count[i][h:h + 1, :], c1)
            rank2 = rank2 + jnp.where(b[i][h:h + 1, :] > s2, 1.0, 0.0)
        c1_ref[:, h, :] = c1
        r2_ref[h] = rank2.astype(r2_ref.dtype)
        e1_ref[:, h, :] = jnp.exp(s1 - a[0][h:h + 1, :]) * inv_z[h:h + 1, :]
        e2_ref[h] = jnp.exp(s2 - b[0][h:h + 1, :]).astype(e2_ref.dtype)


def peer_tables(q, subkeys, n_heads):
    t = q.shape[0]
    n_keys = subkeys.shape[1]
    assert n_keys == LANES and n_heads == SUBLANES
    tt = _tile(t, 256)
    tspec = pl.BlockSpec((n_heads, n_keys, tt), lambda i: (0, 0, i))
    kspec = pl.BlockSpec((n_keys, n_heads, tt), lambda i: (0, 0, i))
    tab = jax.ShapeDtypeStruct((n_heads, n_keys, t), BF16)
    ktab = jax.ShapeDtypeStruct((n_keys, n_heads, t), F32)
    return pl.pallas_call(
        _peer_tables_kernel,
        grid=(t // tt,),
        in_specs=[pl.BlockSpec((tt, q.shape[1]), lambda i: (i, 0)),
                  pl.BlockSpec(subkeys.shape, lambda i: (0, 0, 0))],
        out_specs=[kspec, kspec, tspec, tspec],
        out_shape=[ktab, ktab, tab, tab],
        scratch_shapes=[pltpu.VMEM((2, PEER_TOPK, n_heads, tt), F32),
                        pltpu.VMEM((2, n_heads, n_keys, tt), F32),
                        pltpu.VMEM((n_heads, n_keys, tt), F32)],
        compiler_params=_cparams("arbitrary"),
        name="peer_tables",
    )(q, subkeys)


def _peer_main_kernel(h_ref, u_ref, v_ref, c1_ref, e1_ref, r2_ref, e2_ref, res_ref, gate_ref, o_ref, w_ref):
    e = pl.program_id(1)
    groups, n_heads, tm = c1_ref.shape
    n_keys = r2_ref.shape[1]

    @pl.when(e == 0)
    def _():
        o_ref[...] = jnp.zeros_like(o_ref)

    act = _gelu(_dot_nt(u_ref[...], h_ref[...]))
    for c in range(tm // LANES):
        cs = slice(c * LANES, (c + 1) * LANES)
        for a0 in range(0, groups, 2):
            accs = [None, None]
            for h in range(n_heads):
                r2c = r2_ref[h, :, cs]
                e2c = e2_ref[h, :, cs]
                for k in range(2):
                    c1row = c1_ref[a0 + k, h:h + 1, cs].astype(BF16)
                    e1row = e1_ref[a0 + k, h:h + 1, cs].astype(BF16)
                    term = jnp.where(r2c < c1row, e2c, jnp.zeros_like(e2c)) * e1row
                    accs[k] = term if accs[k] is None else accs[k] + term
            for k in range(2):
                w_ref[(a0 + k) * n_keys:(a0 + k + 1) * n_keys, cs] = accs[k].astype(F32)
    o_ref[...] += _dot((w_ref[...] * act).T.astype(BF16), v_ref[...])

    @pl.when(e == pl.num_programs(1) - 1)
    def _():
        o_ref[...] = res_ref[...] + gate_ref[...] * o_ref[...]


def peer_main(h, u, v, layer, tables, res, gate, rows_per_src):
    c1, e1, r2, e2 = tables
    t, d = h.shape
    n_exp = u.shape[1]
    n_heads, n_keys, _ = r2.shape
    tm = _tile(rows_per_src, 512)
    groups = 8
    te = groups * n_keys
    tok = pl.BlockSpec((tm, d), lambda i, e: (i, 0))
    exp = pl.BlockSpec((None, te, d), lambda i, e: (layer, e, 0))
    grp = pl.BlockSpec((groups, n_heads, tm), lambda i, e: (e, 0, i))
    full = pl.BlockSpec((n_heads, n_keys, tm), lambda i, e: (0, 0, i))
    return pl.pallas_call(
        _peer_main_kernel,
        grid=(t // tm, n_exp // te),
        in_specs=[tok, exp, exp, grp, grp, full, full, tok,
                  pl.BlockSpec((None, 1, d), lambda i, e: (i * tm // rows_per_src, 0, 0))],
        out_specs=tok,
        out_shape=jax.ShapeDtypeStruct((t, d), F32),
        scratch_shapes=[pltpu.VMEM((te, tm), F32)],
        compiler_params=_cparams("arbitrary", "arbitrary"),
        name="peer_main",
    )(h, u, v, c1, e1, r2, e2, res, gate)


def _rope_tables(seq, head_dim):
    rows = seq // GRID_W
    axis_dim = head_dim // 2
    r = jnp.repeat(jnp.arange(rows, dtype=F32), GRID_W)
    col = jnp.tile(jnp.arange(GRID_W, dtype=F32), rows)
    inv = ROPE_BASE ** (-jnp.arange(0, axis_dim, 2, dtype=F32) / axis_dim)
    ang = jnp.concatenate([r[:, None] * inv, col[:, None] * inv], axis=-1)
    cos, sin = jnp.cos(ang), jnp.sin(ang)
    reps = LANES // head_dim
    cos_l = jnp.tile(jnp.concatenate([cos, cos], axis=-1), (1, reps))
    sin_l = jnp.tile(jnp.concatenate([-sin, sin], axis=-1), (1, reps))
    return cos_l, sin_l


def kernel(x, c, ctx, c_ctx, w_ada, b_ada, norm_mix_g, norm_ffn_g, w_in, rg_conv_w, rg_conv_b, rg_wa, rg_ba, rg_wx, rg_bx, rg_lambda, da_lq, da_lk, da_subln_g, hg_lb, hg_onorm_g, w_branch, b_gate, w_out, peer_wq, peer_subkeys, peer_u, peer_v, final_norm_g):
    bsz, s, d = x.shape
    n_ctx = ctx.shape[1]
    depth = w_in.shape[0]
    head_dim = da_lq.shape[-1]
    da_heads = BRANCH_COLS // (2 * head_dim)
    peer_heads = peer_wq.shape[-1] // (2 * peer_subkeys.shape[-1])
    assert bsz + 1 <= SUBLANES and 2 * head_dim == LANES

    xl = x.reshape(bsz * s, d)
    xc = ctx.reshape(bsz * n_ctx, d)
    cvec = jnp.concatenate([c, c_ctx[None, :], jnp.zeros((SUBLANES - bsz - 1, d), F32)], axis=0)
    mods = ada_mods(cvec, w_ada, b_ada).reshape(depth, SUBLANES, N_MOD, d)
    lb = jnp.cumsum(jax.nn.softmax(hg_lb.astype(F32), axis=1), axis=1)
    lb = lb - lb[:, :1]
    cos_l, sin_l = _rope_tables(s, head_dim)
    scale = head_dim ** -0.5 * math.log2(math.e)
    u_bf = peer_u.astype(BF16)
    v_bf = peer_v.astype(BF16)

    for l in range(depth):
        last = l == depth - 1
        ml = [mods[l, :bsz, k][:, None, :] for k in range(N_MOD)]
        mc = [mods[l, bsz:bsz + 1, k][:, None, :] for k in range(N_MOD)]
        lam_init = 0.8 - 0.6 * math.exp(-0.3 * l)
        lam = jnp.exp(jnp.sum(da_lq[l, 0] * da_lk[l, 0])) - jnp.exp(jnp.sum(da_lq[l, 1] * da_lk[l, 1])) + lam_init
        par = jnp.concatenate([jnp.full((1, LANES), lam, F32), da_subln_g[l][None, :],
                               jnp.full((1, LANES), 1.0 - lam_init, F32),
                               jnp.zeros((SUBLANES - 3, LANES), F32)], axis=0)

        hl = norm_mod(xl, norm_mix_g[l], ml[0], ml[1], s)
        hc = norm_mod(xc, norm_mix_g[l], mc[0], mc[1], bsz * n_ctx)
        pj_l = matmul(hl, w_in, l, F32, tn_pref=1024)
        pj_c = matmul(hc, w_in, l, F32, tn_pref=1024)
        rg_l, rg_c = rglru(pj_l, pj_c, bsz, rg_conv_w[l], rg_conv_b[l], rg_wa[l], rg_ba[l], rg_wx[l], rg_bx[l],
                           rg_lambda[l])
        ql, kl, vl = qkv_prep(pj_l, cos_l, sin_l, s, scale, True)
        qc, kc, vc = qkv_prep(pj_c, cos_l, sin_l, n_ctx, scale, False)
        da_l = diff_attention(ql, kc, vc, kl, vl, par, bsz, da_heads)
        hg_l, hg_c = hgrn(pj_l, pj_c, bsz, lb[:, l], hg_onorm_g[l])
        mg_l = merge((rg_l, da_l, hg_l), pj_l, w_branch, l, b_gate[l])
        xl = matmul_residual(mg_l, w_out, l, xl, ml[2], s)
        if not last:
            da_c = diff_attention(qc, kc, vc, None, None, par, bsz, da_heads)
            mg_c = merge((rg_c, da_c, hg_c), pj_c, w_branch, l, b_gate[l])
            xc = matmul_residual(mg_c, w_out, l, xc, mc[2], bsz * n_ctx)

        hl = norm_mod(xl, norm_ffn_g[l], ml[3], ml[4], s)
        tabs = peer_tables(matmul(hl, peer_wq, l, F32), peer_subkeys[l], peer_heads)
        xl = peer_main(hl, u_bf, v_bf, l, tabs, xl, ml[5], s)
        if not last:
            hc = norm_mod(xc, norm_ffn_g[l], mc[3], mc[4], bsz * n_ctx)
            tabs = peer_tables(matmul(hc, peer_wq, l, F32), peer_subkeys[l], peer_heads)
            xc = peer_main(hc, u_bf, v_bf, l, tabs, xc, mc[5], bsz * n_ctx)

    return plain_norm(xl, final_norm_g).reshape(bsz, s, d)
```
